```python
import math
import jax, jax.numpy as jnp
from jax import lax
import numpy as np

D_MODEL = 1024
BATCH = 8
SEQ = 4096
DEPTH = 1

D_MIX = D_MODEL
CONV_WIDTH = D_MIX // 2
ATTN_WIDTH = D_MIX - CONV_WIDTH
HEAD_DIM = 64
N_ATTN_HEADS = ATTN_WIDTH // HEAD_DIM
CONV_K = 3
Q_BLOCK = 128
LN_EPS = 1e-5
DEEPNORM_ALPHA = (2.0 * DEPTH) ** 0.25
DEEPNORM_BETA = (8.0 * DEPTH) ** -0.25
PROJ_WIDTH = 4 * CONV_WIDTH + 4 * ATTN_WIDTH

kernel_name = "hymba_shortconv_stickbreaking_deepnorm"


def _layer_norm(y, gain, bias):
    y32 = y.astype(jnp.float32)
    mu = jnp.mean(y32, axis=-1, keepdims=True)
    var = jnp.mean(jnp.square(y32 - mu), axis=-1, keepdims=True)
    out = (y32 - mu) * lax.rsqrt(var + LN_EPS) * gain.astype(jnp.float32) + bias.astype(jnp.float32)
    return out.astype(y.dtype)


def _short_conv(b_gate, c_gate, h, conv_w):
    u = c_gate * h
    seq = u.shape[1]
    u_pad = jnp.pad(u, ((0, 0), (CONV_K - 1, 0), (0, 0)))
    y = conv_w[CONV_K - 1] * u_pad[:, CONV_K - 1:CONV_K - 1 + seq]
    for tap in range(CONV_K - 1):
        y = y + conv_w[tap] * u_pad[:, tap:tap + seq]
    return b_gate * y


def _stick_breaking_attention(q, k, v):
    bsz, seq, _ = q.shape
    out_dtype = v.dtype
    def heads(t):
        return t.reshape(bsz, seq, N_ATTN_HEADS, HEAD_DIM).transpose(0, 2, 1, 3).astype(jnp.float32)
    qh = heads(q) * (HEAD_DIM ** -0.5)
    kh, vh = heads(k), heads(v)
    n_blocks = seq // Q_BLOCK
    outs = []
    for blk in range(n_blocks):
        q0 = blk * Q_BLOCK
        kv_len = q0 + Q_BLOCK
        qb = qh[:, :, q0:kv_len]
        kb = kh[:, :, :kv_len]
        vb = vh[:, :, :kv_len]
        z = jnp.einsum('bhqd,bhkd->bhqk', qb, kb)
        t_pos = q0 + jnp.arange(Q_BLOCK)[:, None]
        s_pos = jnp.arange(kv_len)[None, :]
        strict = s_pos < t_pos
        neg_log_keep = jnp.where(strict, jax.nn.softplus(z), 0.0)
        after = lax.cumsum(neg_log_keep, axis=3, reverse=True) - neg_log_keep
        log_w = jax.nn.log_sigmoid(z) - after
        w = jnp.where(strict, jnp.exp(log_w), 0.0)
        outs.append(jnp.einsum('bhqk,bhkd->bhqd', w, vb))
    o = jnp.concatenate(outs, axis=2)
    return o.transpose(0, 2, 1, 3).reshape(bsz, seq, ATTN_WIDTH).astype(out_dtype)


def setup_inputs(seed: int = 0) -> dict:
    key = jax.random.key(seed)
    k_x, k_in, k_conv, k_out, k_g, k_b = jax.random.split(key, 6)
    x = jax.random.normal(k_x, (BATCH, SEQ, D_MODEL), jnp.float32)
    w_in = jax.random.normal(k_in, (DEPTH, D_MODEL, PROJ_WIDTH), jnp.float32) * (D_MODEL ** -0.5)
    conv_w = jax.random.normal(k_conv, (DEPTH, CONV_K, CONV_WIDTH), jnp.float32) * (CONV_K ** -0.5)
    w_out = jax.random.normal(k_out, (DEPTH, D_MIX, D_MODEL), jnp.float32) * (D_MIX ** -0.5) * DEEPNORM_BETA
    ln_gain = 1.0 + 0.05 * jax.random.normal(k_g, (DEPTH, D_MODEL), jnp.float32)
    ln_bias = 0.02 * jax.random.normal(k_b, (DEPTH, D_MODEL), jnp.float32)
    return {"x": x, "w_in": w_in, "conv_w": conv_w, "w_out": w_out,
            "ln_gain": ln_gain, "ln_bias": ln_bias}


def reference(x, w_in, conv_w, w_out, ln_gain, ln_bias):
    h_res = x
    for layer in range(DEPTH):
        proj = jnp.einsum('bsd,de->bse', h_res, w_in[layer])
        b_c, c_c, h_c, z_c, q, k, v, z_a = jnp.split(proj, 8, axis=-1)
        conv_out = jax.nn.silu(z_c) * _short_conv(b_c, c_c, h_c, conv_w[layer])
        attn_out = jax.nn.silu(z_a) * _stick_breaking_attention(q, k, v)
        mix = jnp.concatenate([conv_out, attn_out], axis=-1)
        sub = jnp.einsum('bse,ed->bsd', mix, w_out[layer])
        h_res = _layer_norm(DEEPNORM_ALPHA * h_res + sub, ln_gain[layer], ln_bias[layer])
    return h_res
```

```python
import functools
import math

import jax
import jax.numpy as jnp
from jax import lax
from jax.experimental import pallas as pl
from jax.experimental.pallas import tpu as pltpu

HEAD_DIM = 64
CONV_K = 3
LN_EPS = 1e-5
LANES = 128
SUBLANES = 8
ROW_TILE = 512
Q_TILE = 256
K_TILE = 256
VMEM_LIMIT = 48 * 1024 * 1024
LOG2E = 1.4426950408889634
INV_LN2 = 1.4426950408889634

F32 = jnp.float32
BF16 = jnp.bfloat16


def _sigmoid(z):
    return 1.0 / (1.0 + jnp.exp(-z))


def _in_proj_kernel(x_ref, w_ref, cw_ref, conv_ref, q_ref, k_ref, v_ref, g_ref, ubuf_ref,
                    *, tiles_per_seq, width):
    tm = x_ref.shape[0]
    xb = x_ref[...].astype(BF16)

    def col(c):
        return jnp.dot(xb, w_ref[:, c * width:(c + 1) * width], preferred_element_type=F32)

    @pl.when(pl.program_id(0) % tiles_per_seq == 0)
    def _():
        ubuf_ref[0:SUBLANES, :] = jnp.zeros((SUBLANES, width), F32)

    u = col(1) * col(2)
    ubuf_ref[SUBLANES:SUBLANES + tm, :] = u
    u1 = ubuf_ref[SUBLANES - 1:SUBLANES - 1 + tm, :]
    u2 = ubuf_ref[SUBLANES - 2:SUBLANES - 2 + tm, :]
    y = cw_ref[2:3, :] * u + cw_ref[0:1, :] * u2 + cw_ref[1:2, :] * u1
    ubuf_ref[0:SUBLANES, :] = ubuf_ref[tm:tm + SUBLANES, :]

    z_c = col(3)
    conv_ref[...] = ((z_c * _sigmoid(z_c)) * (col(0) * y)).astype(conv_ref.dtype)

    q_ref[...] = (col(4) * (LOG2E * HEAD_DIM ** -0.5)).astype(q_ref.dtype)
    k_ref[...] = col(5).astype(k_ref.dtype)
    v_ref[...] = col(6).astype(v_ref.dtype)
    z_a = col(7)
    g_ref[...] = (z_a * _sigmoid(z_a)).astype(g_ref.dtype)


def _in_proj(x2, w_in_b, conv_w, seq):
    n, d = x2.shape
    width = w_in_b.shape[1] // 8
    tm = ROW_TILE
    out = jax.ShapeDtypeStruct((n, width), BF16)
    row_spec = pl.BlockSpec((tm, width), lambda i: (i, 0))
    return pl.pallas_call(
        functools.partial(_in_proj_kernel, tiles_per_seq=seq // tm, width=width),
        grid=(n // tm,),
        in_specs=[
            pl.BlockSpec((tm, d), lambda i: (i, 0)),
            pl.BlockSpec(w_in_b.shape, lambda i: (0, 0)),
            pl.BlockSpec(conv_w.shape, lambda i: (0, 0)),
        ],
        out_specs=[row_spec] * 5,
        out_shape=[out] * 5,
        scratch_shapes=[pltpu.VMEM((tm + 2 * SUBLANES, width), F32)],
        compiler_params=pltpu.CompilerParams(
            dimension_semantics=("arbitrary",), vmem_limit_bytes=VMEM_LIMIT),
        name="in_proj",
    )(x2, w_in_b, conv_w)


def _attn_kernel(q_ref, k_ref, v_ref, g_ref, o_ref):
    tq = q_ref.shape[0]
    i = pl.program_id(2)

    lane = lax.broadcasted_iota(jnp.int32, (tq, LANES), 1)
    q2 = q_ref[...]
    zero = jnp.zeros_like(q2)
    q_heads = (jnp.where(lane < HEAD_DIM, q2, zero), jnp.where(lane >= HEAD_DIM, q2, zero))

    row = lax.broadcasted_iota(jnp.int32, (K_TILE, K_TILE), 0)
    colm = lax.broadcasted_iota(jnp.int32, (K_TILE, K_TILE), 1)
    tri = jnp.where(row >= colm, 1.0, 0.0).astype(BF16)
    strict = lax.broadcasted_iota(jnp.int32, (tq, K_TILE), 1) < lax.broadcasted_iota(
        jnp.int32, (tq, K_TILE), 0)

    def tile(qh, kj, vj, carry, acc, mask):
        z = lax.dot_general(qh, kj, (((1,), (1,)), ((), ())), preferred_element_type=F32)
        sp = jnp.maximum(z, 0.0) + jnp.log(1.0 + jnp.exp2(-jnp.abs(z))) * INV_LN2
        if mask is not None:
            sp = jnp.where(mask, sp, 0.0)
        c = jnp.dot(sp.astype(BF16), tri, preferred_element_type=F32)
        c = c + jnp.concatenate([carry, carry], axis=1)
        w = jnp.exp2(z - c)
        if mask is not None:
            w = jnp.where(mask, w, 0.0)
        acc = acc + jnp.dot(w.astype(BF16), vj, preferred_element_type=F32)
        carry = jnp.broadcast_to(c[:, 0:1], (tq, LANES))
        return carry, acc

    def kv(j):
        start = pl.multiple_of(j * K_TILE, K_TILE)
        return k_ref[pl.ds(start, K_TILE), :], v_ref[pl.ds(start, K_TILE), :]

    zeros = jnp.zeros((tq, LANES), F32)
    kd, vd = kv(i)
    state = []
    for qh in q_heads:
        state.extend(tile(qh, kd, vd, zeros, zeros, strict))

    def body(jj, st):
        kj, vj = kv(i - 1 - jj)
        c0, a0 = tile(q_heads[0], kj, vj, st[0], st[1], None)
        c1, a1 = tile(q_heads[1], kj, vj, st[2], st[3], None)
        return (c0, a0, c1, a1)

    st = lax.fori_loop(0, i, body, tuple(state))
    o = jnp.where(lane < HEAD_DIM, st[1], st[3])
    o_ref[...] = (o * g_ref[...].astype(F32)).astype(o_ref.dtype)


def _attention(q, k, v, g, batch, seq):
    n, width = q.shape
    n_pairs = width // LANES
    n_q = seq // Q_TILE
    q_spec = pl.BlockSpec((Q_TILE, LANES), lambda b, hp, i: (b * n_q + i, hp))
    kv_spec = pl.BlockSpec((seq, LANES), lambda b, hp, i: (b, hp))
    return pl.pallas_call(
        _attn_kernel,
        grid=(batch, n_pairs, n_q),
        in_specs=[q_spec, kv_spec, kv_spec, q_spec],
        out_specs=q_spec,
        out_shape=jax.ShapeDtypeStruct((n, width), BF16),
        compiler_params=pltpu.CompilerParams(
            dimension_semantics=("arbitrary", "arbitrary", "arbitrary"),
            vmem_limit_bytes=VMEM_LIMIT),
        name="stick_attn",
    )(q, k, v, g)


def _out_proj_kernel(conv_ref, attn_ref, x_ref, w_ref, gain_ref, bias_ref, o_ref, *, alpha):
    half = conv_ref.shape[1]
    sub = jnp.dot(conv_ref[...], w_ref[0:half, :], preferred_element_type=F32)
    sub = sub + jnp.dot(attn_ref[...], w_ref[half:2 * half, :], preferred_element_type=F32)
    y = alpha * x_ref[...] + sub
    mu = jnp.mean(y, axis=-1, keepdims=True)
    yc = y - mu
    var = jnp.mean(yc * yc, axis=-1, keepdims=True)
    o_ref[...] = yc * lax.rsqrt(var + LN_EPS) * gain_ref[...] + bias_ref[...]


def _out_proj(conv, attn, x2, w_out_b, gain, bias, alpha):
    n, d = x2.shape
    half = conv.shape[1]
    tm = ROW_TILE
    return pl.pallas_call(
        functools.partial(_out_proj_kernel, alpha=alpha),
        grid=(n // tm,),
        in_specs=[
            pl.BlockSpec((tm, half), lambda i: (i, 0)),
            pl.BlockSpec((tm, half), lambda i: (i, 0)),
            pl.BlockSpec((tm, d), lambda i: (i, 0)),
            pl.BlockSpec(w_out_b.shape, lambda i: (0, 0)),
            pl.BlockSpec((1, d), lambda i: (0, 0)),
            pl.BlockSpec((1, d), lambda i: (0, 0)),
        ],
        out_specs=pl.BlockSpec((tm, d), lambda i: (i, 0)),
        out_shape=jax.ShapeDtypeStruct((n, d), F32),
        compiler_params=pltpu.CompilerParams(
            dimension_semantics=("arbitrary",), vmem_limit_bytes=VMEM_LIMIT),
        name="out_proj",
    )(conv, attn, x2, w_out_b, gain, bias)


def kernel(x, w_in, conv_w, w_out, ln_gain, ln_bias):
    batch, seq, d_model = x.shape
    depth = w_in.shape[0]
    alpha = (2.0 * depth) ** 0.25
    h = x.reshape(batch * seq, d_model)
    for layer in range(depth):
        conv, q, k, v, g = _in_proj(h, w_in[layer].astype(BF16), conv_w[layer], seq)
        attn = _attention(q, k, v, g, batch, seq)
        h = _out_proj(conv, attn, h, w_out[layer].astype(BF16),
                      ln_gain[layer][None, :], ln_bias[layer][None, :], alpha)
    return h.reshape(batch, seq, d_model)
```

```python
import functools

import jax
import jax.numpy as jnp
import numpy as np
from jax import lax
from jax.experimental import pallas as pl
from jax.experimental.pallas import tpu as pltpu

HEAD_DIM = 64
LN_EPS = 1e-5
LANES = 128
SUBLANES = 8
ROW_TILE = 512
ATT_TILE = 256
VMEM_LIMIT = 48 * 1024 * 1024
LOG2E = 1.4426950408889634
INV_LN2 = 1.4426950408889634
MASKED_LOGIT = -1e30

F32 = jnp.float32
BF16 = jnp.bfloat16


def _sigmoid(z):
    return 1.0 / (1.0 + jnp.exp(-z))


def _in_proj_kernel(x_ref, w_ref, cw_ref, conv_ref, q_ref, k_ref, v_ref, g_ref, ubuf_ref,
                    *, tiles_per_seq, width):
    tm = x_ref.shape[0]
    xb = x_ref[...].astype(BF16)

    def col(c):
        return jnp.dot(xb, w_ref[:, c * width:(c + 1) * width], preferred_element_type=F32)

    @pl.when(pl.program_id(0) % tiles_per_seq == 0)
    def _():
        ubuf_ref[0:SUBLANES, :] = jnp.zeros((SUBLANES, width), F32)

    u = col(1) * col(2)
    ubuf_ref[SUBLANES:SUBLANES + tm, :] = u
    u1 = ubuf_ref[SUBLANES - 1:SUBLANES - 1 + tm, :]
    u2 = ubuf_ref[SUBLANES - 2:SUBLANES - 2 + tm, :]
    y = cw_ref[2:3, :] * u + cw_ref[0:1, :] * u2 + cw_ref[1:2, :] * u1
    ubuf_ref[0:SUBLANES, :] = ubuf_ref[tm:tm + SUBLANES, :]

    z_c = col(3)
    conv_ref[...] = ((z_c * _sigmoid(z_c)) * (col(0) * y)).astype(conv_ref.dtype)

    q_ref[...] = (col(4) * (LOG2E * HEAD_DIM ** -0.5)).astype(q_ref.dtype)
    k_ref[...] = col(5).astype(k_ref.dtype)
    v_ref[...] = col(6).astype(v_ref.dtype)
    z_a = col(7)
    g_ref[...] = (z_a * _sigmoid(z_a)).astype(g_ref.dtype)


def _in_proj(x2, w_in_b, conv_w, seq):
    n, d = x2.shape
    width = w_in_b.shape[1] // 8
    tm = ROW_TILE
    out = jax.ShapeDtypeStruct((n, width), BF16)
    row_spec = pl.BlockSpec((tm, width), lambda i: (i, 0))
    return pl.pallas_call(
        functools.partial(_in_proj_kernel, tiles_per_seq=seq // tm, width=width),
        grid=(n // tm,),
        in_specs=[
            pl.BlockSpec((tm, d), lambda i: (i, 0)),
            pl.BlockSpec(w_in_b.shape, lambda i: (0, 0)),
            pl.BlockSpec(conv_w.shape, lambda i: (0, 0)),
        ],
        out_specs=[row_spec] * 5,
        out_shape=[out] * 5,
        scratch_shapes=[pltpu.VMEM((tm + 2 * SUBLANES, width), F32)],
        compiler_params=pltpu.CompilerParams(
            dimension_semantics=("arbitrary",), vmem_limit_bytes=VMEM_LIMIT),
        name="in_proj",
    )(x2, w_in_b, conv_w)


def _tile_schedule(n_tiles):
    qi, kj = [], []
    for i in range(n_tiles):
        for j in range(i, -1, -1):
            qi.append(i)
            kj.append(j)
    return np.asarray(qi, np.int32), np.asarray(kj, np.int32)


def _attn_kernel(qi_ref, kj_ref, q_ref, k_ref, v_ref, g_ref, tri_ref, cap_ref, o_ref,
                 z_buf, d_buf, tot_buf, carry_ref, acc_ref, *, n_steps):
    t = ATT_TILE
    lane = lax.broadcasted_iota(jnp.int32, (t, LANES), 1)
    head_lanes = (lane < HEAD_DIM, lane >= HEAD_DIM)
    sign_bit = jnp.uint32(0x80000000)

    def rows(i):
        return pl.ds(pl.multiple_of(i * t, t), t)

    def logits(s, slot):
        qi, kj = qi_ref[s], kj_ref[s]
        q2 = q_ref[rows(qi), :]
        kt = k_ref[rows(kj), :]
        cap = cap_ref[jnp.where(qi == kj, 1, 0)]
        for h in range(2):
            qh = jnp.where(head_lanes[h], q2, jnp.zeros_like(q2))
            z = lax.dot_general(qh, kt, (((1,), (1,)), ((), ())), preferred_element_type=F32)
            z_buf[slot, h] = jnp.minimum(z, cap)

    def cumsum(slot):
        for h in range(2):
            z = z_buf[slot, h]
            neg_abs = pltpu.bitcast(pltpu.bitcast(z, jnp.uint32) | sign_bit, F32)
            sp = jnp.maximum(z, 0.0) + jnp.log(1.0 + jnp.exp2(neg_abs)) * INV_LN2
            cin = jnp.dot(sp.astype(BF16), tri_ref[...], preferred_element_type=F32)
            d_buf[slot, h] = z - cin
            tot_buf[slot, h] = jnp.broadcast_to(cin[:, 0:1], (t, LANES))

    def output(s, slot):
        qi, kj = qi_ref[s], kj_ref[s]
        first = qi == kj
        vt = v_ref[rows(kj), :]
        for h in range(2):
            carry = jnp.where(first, 0.0, carry_ref[h])
            w = jnp.exp2(d_buf[slot, h] - jnp.concatenate([carry, carry], axis=1))
            carry_ref[h] = carry + tot_buf[slot, h]
            pv = jnp.dot(w.astype(BF16), vt, preferred_element_type=F32)
            acc_ref[h, rows(qi), :] += pv

    carry_ref[...] = jnp.zeros_like(carry_ref)
    acc_ref[...] = jnp.zeros_like(acc_ref)

    logits(0, 0)
    cumsum(0)
    logits(1, 1)

    def pair(u, carry):
        s = 2 + 2 * u
        output(s - 2, 0)
        cumsum(1)
        logits(s, 0)
        output(s - 1, 1)
        cumsum(0)
        logits(s + 1, 1)
        return carry

    lax.fori_loop(0, (n_steps - 2) // 2, pair, 0)
    output(n_steps - 2, 0)
    cumsum(1)
    output(n_steps - 1, 1)

    def finish(i, carry):
        r = rows(i)
        o = jnp.where(head_lanes[0], acc_ref[0, r, :], acc_ref[1, r, :])
        o_ref[r, :] = (o * g_ref[r, :].astype(F32)).astype(o_ref.dtype)
        return carry

    lax.fori_loop(0, q_ref.shape[0] // t, finish, 0)


def _attention(q, k, v, g, batch, seq):
    n, width = q.shape
    n_pairs = width // LANES
    t = ATT_TILE
    qi_tab, kj_tab = _tile_schedule(seq // t)
    n_steps = len(qi_tab)
    assert n_steps % 2 == 0 and n_steps >= 4

    idx = np.arange(t)
    tri = jnp.asarray(idx[:, None] >= idx[None, :], BF16)
    strict = idx[None, :] < idx[:, None]
    cap = jnp.asarray(np.stack([np.full((t, t), np.inf, np.float32),
                                np.where(strict, np.inf, MASKED_LOGIT).astype(np.float32)]))

    seq_spec = pl.BlockSpec((seq, LANES), lambda b, hp, *_: (b, hp))
    grid_spec = pltpu.PrefetchScalarGridSpec(
        num_scalar_prefetch=2,
        grid=(batch, n_pairs),
        in_specs=[seq_spec, seq_spec, seq_spec, seq_spec,
                  pl.BlockSpec((t, t), lambda b, hp, *_: (0, 0)),
                  pl.BlockSpec((2, t, t), lambda b, hp, *_: (0, 0, 0))],
        out_specs=seq_spec,
        scratch_shapes=[
            pltpu.VMEM((2, 2, t, t), F32),
            pltpu.VMEM((2, 2, t, t), F32),
            pltpu.VMEM((2, 2, t, LANES), F32),
            pltpu.VMEM((2, t, LANES), F32),
            pltpu.VMEM((2, seq, LANES), F32),
        ],
    )
    return pl.pallas_call(
        functools.partial(_attn_kernel, n_steps=n_steps),
        grid_spec=grid_spec,
        out_shape=jax.ShapeDtypeStruct((n, width), BF16),
        compiler_params=pltpu.CompilerParams(
            dimension_semantics=("arbitrary", "arbitrary"), vmem_limit_bytes=VMEM_LIMIT),
        name="stick_attn",
    )(jnp.asarray(qi_tab), jnp.asarray(kj_tab), q, k, v, g, tri, cap)


def _out_proj_kernel(conv_ref, attn_ref, x_ref, w_ref, gain_ref, bias_ref, o_ref, *, alpha):
    half = conv_ref.shape[1]
    sub = jnp.dot(conv_ref[...], w_ref[0:half, :], preferred_element_type=F32)
    sub = sub + jnp.dot(attn_ref[...], w_ref[half:2 * half, :], preferred_element_type=F32)
    y = alpha * x_ref[...] + sub
    mu = jnp.mean(y, axis=-1, keepdims=True)
    yc = y - mu
    var = jnp.mean(yc * yc, axis=-1, keepdims=True)
    o_ref[...] = yc * lax.rsqrt(var + LN_EPS) * gain_ref[...] + bias_ref[...]


def _out_proj(conv, attn, x2, w_out_b, gain, bias, alpha):
    n, d = x2.shape
    half = conv.shape[1]
    tm = ROW_TILE
    return pl.pallas_call(
        functools.partial(_out_proj_kernel, alpha=alpha),
        grid=(n // tm,),
        in_specs=[
            pl.BlockSpec((tm, half), lambda i: (i, 0)),
            pl.BlockSpec((tm, half), lambda i: (i, 0)),
            pl.BlockSpec((tm, d), lambda i: (i, 0)),
            pl.BlockSpec(w_out_b.shape, lambda i: (0, 0)),
            pl.BlockSpec((1, d), lambda i: (0, 0)),
            pl.BlockSpec((1, d), lambda i: (0, 0)),
        ],
        out_specs=pl.BlockSpec((tm, d), lambda i: (i, 0)),
        out_shape=jax.ShapeDtypeStruct((n, d), F32),
        compiler_params=pltpu.CompilerParams(
            dimension_semantics=("arbitrary",), vmem_limit_bytes=VMEM_LIMIT),
        name="out_proj",
    )(conv, attn, x2, w_out_b, gain, bias)


def kernel(x, w_in, conv_w, w_out, ln_gain, ln_bias):
    batch, seq, d_model = x.shape
    depth = w_in.shape[0]
    alpha = (2.0 * depth) ** 0.25
    h = x.reshape(batch * seq, d_model)
    for layer in range(depth):
        conv, q, k, v, g = _in_proj(h, w_in[layer].astype(BF16), conv_w[layer], seq)
        attn = _attention(q, k, v, g, batch, seq)
        h = _out_proj(conv, attn, h, w_out[layer].astype(BF16),
                      ln_gain[layer][None, :], ln_bias[layer][None, :], alpha)
    return h.reshape(batch, seq, d_model)
```

```python
import functools

import jax
import jax.numpy as jnp
import numpy as np
from jax import lax
from jax.experimental import pallas as pl
from jax.experimental.pallas import tpu as pltpu

HEAD_DIM = 64
LN_EPS = 1e-5
LANES = 128
SUBLANES = 8
ROW_TILE = 512
ATT_TILE = 256
VMEM_LIMIT = 48 * 1024 * 1024
LOG2E = 1.4426950408889634
INV_LN2 = 1.4426950408889634
MASKED_LOGIT = -1e30
DEAD_CARRY = 160.0

F32 = jnp.float32
BF16 = jnp.bfloat16


def _sigmoid(z):
    return 1.0 / (1.0 + jnp.exp(-z))


def _in_proj_kernel(x_ref, w_ref, cw_ref, conv_ref, q_ref, k_ref, v_ref, g_ref, ubuf_ref,
                    *, tiles_per_seq, width):
    tm = x_ref.shape[0]
    xb = x_ref[...].astype(BF16)

    def col(c):
        return jnp.dot(xb, w_ref[:, c * width:(c + 1) * width], preferred_element_type=F32)

    @pl.when(pl.program_id(0) % tiles_per_seq == 0)
    def _():
        ubuf_ref[0:SUBLANES, :] = jnp.zeros((SUBLANES, width), F32)

    u = col(1) * col(2)
    ubuf_ref[SUBLANES:SUBLANES + tm, :] = u
    u1 = ubuf_ref[SUBLANES - 1:SUBLANES - 1 + tm, :]
    u2 = ubuf_ref[SUBLANES - 2:SUBLANES - 2 + tm, :]
    y = cw_ref[2:3, :] * u + cw_ref[0:1, :] * u2 + cw_ref[1:2, :] * u1
    ubuf_ref[0:SUBLANES, :] = ubuf_ref[tm:tm + SUBLANES, :]

    z_c = col(3)
    conv_ref[...] = ((z_c * _sigmoid(z_c)) * (col(0) * y)).astype(conv_ref.dtype)

    q_ref[...] = (col(4) * (LOG2E * HEAD_DIM ** -0.5)).astype(q_ref.dtype)
    k_ref[...] = col(5).astype(k_ref.dtype)
    v_ref[...] = col(6).astype(v_ref.dtype)
    z_a = col(7)
    g_ref[...] = (z_a * _sigmoid(z_a)).astype(g_ref.dtype)


def _in_proj(x2, w_in_b, conv_w, seq):
    n, d = x2.shape
    width = w_in_b.shape[1] // 8
    tm = ROW_TILE
    out = jax.ShapeDtypeStruct((n, width), BF16)
    row_spec = pl.BlockSpec((tm, width), lambda i: (i, 0))
    return pl.pallas_call(
        functools.partial(_in_proj_kernel, tiles_per_seq=seq // tm, width=width),
        grid=(n // tm,),
        in_specs=[
            pl.BlockSpec((tm, d), lambda i: (i, 0)),
            pl.BlockSpec(w_in_b.shape, lambda i: (0, 0)),
            pl.BlockSpec(conv_w.shape, lambda i: (0, 0)),
        ],
        out_specs=[row_spec] * 5,
        out_shape=[out] * 5,
        scratch_shapes=[pltpu.VMEM((tm + 2 * SUBLANES, width), F32)],
        compiler_params=pltpu.CompilerParams(
            dimension_semantics=("arbitrary",), vmem_limit_bytes=VMEM_LIMIT),
        name="in_proj",
    )(x2, w_in_b, conv_w)


def _near_schedule(n_tiles):
    qi = list(range(n_tiles)) + list(range(1, n_tiles))
    kj = list(range(n_tiles)) + list(range(0, n_tiles - 1))
    return np.asarray(qi, np.int32), np.asarray(kj, np.int32)


def _attn_kernel(qi_ref, kj_ref, q_ref, k_ref, v_ref, g_ref, tri_ref, cap_ref, o_ref,
                 z_buf, d_buf, tot_buf, carry_ref, acc_ref, *, n_diag, n_steps):
    t = ATT_TILE
    n_q = q_ref.shape[0] // t
    lane = lax.broadcasted_iota(jnp.int32, (t, LANES), 1)
    head_lanes = (lane < HEAD_DIM, lane >= HEAD_DIM)

    def rows(i):
        return pl.ds(pl.multiple_of(i * t, t), t)

    def head_logits(q2, kt, h):
        qh = jnp.where(head_lanes[h], q2, jnp.zeros_like(q2))
        return lax.dot_general(qh, kt, (((1,), (1,)), ((), ())), preferred_element_type=F32)

    def softplus_cumsum(z):
        sp = jnp.maximum(z, 0.0) + jnp.log(1.0 + jnp.exp2(-jnp.abs(z))) * INV_LN2
        cin = jnp.dot(sp.astype(BF16), tri_ref[...], preferred_element_type=F32)
        return z - cin, jnp.broadcast_to(cin[:, 0:1], (t, LANES))

    def accumulate(h, qi, d, tot, vt):
        carry = carry_ref[h, rows(qi), :]
        w = jnp.exp2(d - jnp.concatenate([carry, carry], axis=1))
        carry_ref[h, rows(qi), :] = carry + tot
        acc_ref[h, rows(qi), :] += jnp.dot(w.astype(BF16), vt, preferred_element_type=F32)

    def logits(s, slot, diag):
        q2 = q_ref[rows(qi_ref[s]), :]
        kt = k_ref[rows(kj_ref[s]), :]
        for h in range(2):
            z = head_logits(q2, kt, h)
            z_buf[slot, h] = jnp.minimum(z, cap_ref[...]) if diag else z

    def cumsum(slot):
        for h in range(2):
            d_buf[slot, h], tot_buf[slot, h] = softplus_cumsum(z_buf[slot, h])

    def output(s, slot):
        vt = v_ref[rows(kj_ref[s]), :]
        for h in range(2):
            accumulate(h, qi_ref[s], d_buf[slot, h], tot_buf[slot, h], vt)

    def trip(s, slot, diag):
        output(s - 2, slot)
        cumsum(1 - slot)
        logits(s, slot, diag)

    def full_trips(lo, hi, diag):
        if lo < hi and lo % 2:
            trip(lo, 1, diag)
            lo += 1
        n_pairs = (hi - lo) // 2
        if n_pairs:
            def pair(u, c, lo=lo):
                s = lo + 2 * u
                trip(s, 0, diag)
                trip(s + 1, 1, diag)
                return c
            lax.fori_loop(0, n_pairs, pair, 0)
            lo += 2 * n_pairs
        if lo < hi:
            trip(lo, lo % 2, diag)

    carry_ref[...] = jnp.zeros_like(carry_ref)
    acc_ref[...] = jnp.zeros_like(acc_ref)

    logits(0, 0, True)
    cumsum(0)
    logits(1, 1, True)
    full_trips(2, n_diag, True)
    full_trips(n_diag, n_steps, False)
    output(n_steps - 2, n_steps % 2)
    cumsum((n_steps - 1) % 2)
    output(n_steps - 1, (n_steps - 1) % 2)

    def far_tiles(qi, c):
        def alive(state):
            kj, low = state
            return jnp.logical_and(kj >= 0, low < DEAD_CARRY)

        def visit(state):
            kj, _ = state
            q2 = q_ref[rows(qi), :]
            kt = k_ref[rows(kj), :]
            vt = v_ref[rows(kj), :]
            for h in range(2):
                d, tot = softplus_cumsum(head_logits(q2, kt, h))
                accumulate(h, qi, d, tot, vt)
            return kj - 1, jnp.min(carry_ref[:, rows(qi), :])

        lax.while_loop(alive, visit, (qi - 2, jnp.min(carry_ref[:, rows(qi), :])))
        return c

    if n_q > 2:
        @pl.when(jnp.min(carry_ref[:, 2 * t:, :]) < DEAD_CARRY)
        def _():
            lax.fori_loop(2, n_q, far_tiles, 0)

    def finish(i, c):
        r = rows(i)
        o = jnp.where(head_lanes[0], acc_ref[0, r, :], acc_ref[1, r, :])
        o_ref[r, :] = (o * g_ref[r, :].astype(F32)).astype(o_ref.dtype)
        return c

    lax.fori_loop(0, n_q, finish, 0)


def _attention(q, k, v, g, batch, seq):
    n, width = q.shape
    n_pairs = width // LANES
    t = ATT_TILE
    n_diag = seq // t
    qi_tab, kj_tab = _near_schedule(n_diag)
    n_steps = len(qi_tab)
    assert n_diag >= 2

    idx = np.arange(t)
    tri = jnp.asarray(idx[:, None] >= idx[None, :], BF16)
    strict = idx[None, :] < idx[:, None]
    cap = jnp.asarray(np.where(strict, np.inf, MASKED_LOGIT).astype(np.float32))

    seq_spec = pl.BlockSpec((seq, LANES), lambda b, hp, *_: (b, hp))
    tile_spec = pl.BlockSpec((t, t), lambda b, hp, *_: (0, 0))
    grid_spec = pltpu.PrefetchScalarGridSpec(
        num_scalar_prefetch=2,
        grid=(batch, n_pairs),
        in_specs=[seq_spec, seq_spec, seq_spec, seq_spec, tile_spec, tile_spec],
        out_specs=seq_spec,
        scratch_shapes=[
            pltpu.VMEM((2, 2, t, t), F32),
            pltpu.VMEM((2, 2, t, t), F32),
            pltpu.VMEM((2, 2, t, LANES), F32),
            pltpu.VMEM((2, seq, LANES), F32),
            pltpu.VMEM((2, seq, LANES), F32),
        ],
    )
    return pl.pallas_call(
        functools.partial(_attn_kernel, n_diag=n_diag, n_steps=n_steps),
        grid_spec=grid_spec,
        out_shape=jax.ShapeDtypeStruct((n, width), BF16),
        compiler_params=pltpu.CompilerParams(
            dimension_semantics=("arbitrary", "arbitrary"), vmem_limit_bytes=VMEM_LIMIT),
        name="stick_attn",
    )(jnp.asarray(qi_tab), jnp.asarray(kj_tab), q, k, v, g, tri, cap)


def _out_proj_kernel(conv_ref, attn_ref, x_ref, w_ref, gain_ref, bias_ref, o_ref, *, alpha):
    half = conv_ref.shape[1]
    sub = jnp.dot(conv_ref[...], w_ref[0:half, :], preferred_element_type=F32)
    sub = sub + jnp.dot(attn_ref[...], w_ref[half:2 * half, :], preferred_element_type=F32)
    y = alpha * x_ref[...] + sub
    mu = jnp.mean(y, axis=-1, keepdims=True)
    yc = y - mu
    var = jnp.mean(yc * yc, axis=-1, keepdims=True)
    o_ref[...] = yc * lax.rsqrt(var + LN_EPS) * gain_ref[...] + bias_ref[...]


def _out_proj(conv, attn, x2, w_out_b, gain, bias, alpha):
    n, d = x2.shape
    half = conv.shape[1]
    tm = ROW_TILE
    return pl.pallas_call(
        functools.partial(_out_proj_kernel, alpha=alpha),
        grid=(n // tm,),
        in_specs=[
            pl.BlockSpec((tm, half), lambda i: (i, 0)),
            pl.BlockSpec((tm, half), lambda i: (i, 0)),
            pl.BlockSpec((tm, d), lambda i: (i, 0)),
            pl.BlockSpec(w_out_b.shape, lambda i: (0, 0)),
            pl.BlockSpec((1, d), lambda i: (0, 0)),
            pl.BlockSpec((1, d), lambda i: (0, 0)),
        ],
        out_specs=pl.BlockSpec((tm, d), lambda i: (i, 0)),
        out_shape=jax.ShapeDtypeStruct((n, d), F32),
        compiler_params=pltpu.CompilerParams(
            dimension_semantics=("arbitrary",), vmem_limit_bytes=VMEM_LIMIT),
        name="out_proj",
    )(conv, attn, x2, w_out_b, gain, bias)


def kernel(x, w_in, conv_w, w_out, ln_gain, ln_bias):
    batch, seq, d_model = x.shape
    depth = w_in.shape[0]
    alpha = (2.0 * depth) ** 0.25
    h = x.reshape(batch * seq, d_model)
    for layer in range(depth):
        conv, q, k, v, g = _in_proj(h, w_in[layer].astype(BF16), conv_w[layer], seq)
        attn = _attention(q, k, v, g, batch, seq)
        h = _out_proj(conv, attn, h, w_out[layer].astype(BF16),
                      ln_gain[layer][None, :], ln_bias[layer][None, :], alpha)
    return h.reshape(batch, seq, d_model)
```

```python
import functools

import jax
import jax.numpy as jnp
import numpy as np
from jax import lax
from jax.experimental import pallas as pl
from jax.experimental.pallas import tpu as pltpu

HEAD_DIM = 64
LN_EPS = 1e-5
LANES = 128
SUBLANES = 8
ROW_TILE = 512
OUT_ROW_TILE = 1024
OUT_CHUNK = 256
ATT_TILE = 256
ATT_SLOTS = 2
VMEM_LIMIT = 48 * 1024 * 1024
LOG2E = 1.4426950408889634
INV_LN2 = 1.4426950408889634
MASKED_LOGIT = -1e30
DEAD_CARRY = 160.0

F32 = jnp.float32
BF16 = jnp.bfloat16


def _sigmoid(z):
    return 1.0 / (1.0 + jnp.exp(-z))


def _in_proj_kernel(x_ref, w_ref, cw_ref, conv_ref, q_ref, k_ref, v_ref, g_ref, ubuf_ref,
                    *, tiles_per_seq, width):
    tm = x_ref.shape[0]
    xb = x_ref[...].astype(BF16)

    def col(c):
        return jnp.dot(xb, w_ref[:, c * width:(c + 1) * width], preferred_element_type=F32)

    @pl.when(pl.program_id(0) % tiles_per_seq == 0)
    def _():
        ubuf_ref[0:SUBLANES, :] = jnp.zeros((SUBLANES, width), F32)

    u = col(1) * col(2)
    ubuf_ref[SUBLANES:SUBLANES + tm, :] = u
    u1 = ubuf_ref[SUBLANES - 1:SUBLANES - 1 + tm, :]
    u2 = ubuf_ref[SUBLANES - 2:SUBLANES - 2 + tm, :]
    y = cw_ref[2:3, :] * u + cw_ref[0:1, :] * u2 + cw_ref[1:2, :] * u1
    ubuf_ref[0:SUBLANES, :] = ubuf_ref[tm:tm + SUBLANES, :]

    z_c = col(3)
    conv_ref[...] = ((z_c * _sigmoid(z_c)) * (col(0) * y)).astype(conv_ref.dtype)

    q_ref[...] = (col(4) * (LOG2E * HEAD_DIM ** -0.5)).astype(q_ref.dtype)
    k_ref[...] = col(5).astype(k_ref.dtype)
    v_ref[...] = col(6).astype(v_ref.dtype)
    z_a = col(7)
    g_ref[...] = (z_a * _sigmoid(z_a)).astype(g_ref.dtype)


def _in_proj(x2, w_in_b, conv_w, seq):
    n, d = x2.shape
    width = w_in_b.shape[1] // 8
    tm = ROW_TILE
    out = jax.ShapeDtypeStruct((n, width), BF16)
    row_spec = pl.BlockSpec((tm, width), lambda i: (i, 0))
    return pl.pallas_call(
        functools.partial(_in_proj_kernel, tiles_per_seq=seq // tm, width=width),
        grid=(n // tm,),
        in_specs=[
            pl.BlockSpec((tm, d), lambda i: (i, 0)),
            pl.BlockSpec(w_in_b.shape, lambda i: (0, 0)),
            pl.BlockSpec(conv_w.shape, lambda i: (0, 0)),
        ],
        out_specs=[row_spec] * 5,
        out_shape=[out] * 5,
        scratch_shapes=[pltpu.VMEM((tm + 2 * SUBLANES, width), F32)],
        compiler_params=pltpu.CompilerParams(
            dimension_semantics=("arbitrary",), vmem_limit_bytes=VMEM_LIMIT),
        name="in_proj",
    )(x2, w_in_b, conv_w)


def _near_schedule(n_tiles):
    return [(i, i) for i in range(n_tiles)] + [(i, i - 1) for i in range(1, n_tiles)]


def _attn_kernel(q_ref, k_ref, v_ref, g_ref, tri_ref, cap_ref, o_ref,
                 z_buf, d_buf, tot_buf, carry_ref, acc_ref):
    t = ATT_TILE
    n_q = q_ref.shape[0] // t
    n_slots = z_buf.shape[0]
    lane = lax.broadcasted_iota(jnp.int32, (t, LANES), 1)
    head_lanes = (lane < HEAD_DIM, lane >= HEAD_DIM)

    def rows(i):
        if isinstance(i, int):
            return slice(i * t, (i + 1) * t)
        return pl.ds(pl.multiple_of(i * t, t), t)

    def head_logits(q2, kt, h):
        qh = jnp.where(head_lanes[h], q2, jnp.zeros_like(q2))
        return lax.dot_general(qh, kt, (((1,), (1,)), ((), ())), preferred_element_type=F32)

    def softplus_cumsum(z):
        sp = jnp.maximum(z, 0.0) + jnp.log(1.0 + jnp.exp2(-jnp.abs(z))) * INV_LN2
        cin = jnp.dot(sp.astype(BF16), tri_ref[...], preferred_element_type=F32)
        return z - cin, jnp.broadcast_to(cin[:, 0:1], (t, LANES))

    def accumulate(h, qi, d, tot, vt, first):
        r = rows(qi)
        if first:
            w = jnp.exp2(d)
            carry_ref[h, r, :] = tot
        else:
            carry = carry_ref[h, r, :]
            w = jnp.exp2(d - jnp.concatenate([carry, carry], axis=1))
            carry_ref[h, r, :] = carry + tot
        pv = jnp.dot(w.astype(BF16), vt, preferred_element_type=F32)
        if first:
            acc_ref[h, r, :] = pv
        else:
            acc_ref[h, r, :] += pv

    steps = _near_schedule(n_q)

    def logits(s):
        qi, kj = steps[s]
        q2 = q_ref[rows(qi), :]
        kt = k_ref[rows(kj), :]
        for h in range(2):
            z = head_logits(q2, kt, h)
            z_buf[s % n_slots, h] = jnp.minimum(z, cap_ref[...]) if qi == kj else z

    def cumsum(s):
        for h in range(2):
            d_buf[s % n_slots, h], tot_buf[s % n_slots, h] = softplus_cumsum(z_buf[s % n_slots, h])

    def output(s):
        qi, kj = steps[s]
        vt = v_ref[rows(kj), :]
        for h in range(2):
            accumulate(h, qi, d_buf[s % n_slots, h], tot_buf[s % n_slots, h], vt, qi == kj)

    for s in range(len(steps) + 2):
        if s >= 2:
            output(s - 2)
        if 1 <= s <= len(steps):
            cumsum(s - 1)
        if s < len(steps):
            logits(s)

    def far_tiles(qi, c):
        def alive(state):
            kj, low = state
            return jnp.logical_and(kj >= 0, low < DEAD_CARRY)

        def visit(state):
            kj, _ = state
            q2 = q_ref[rows(qi), :]
            kt = k_ref[rows(kj), :]
            vt = v_ref[rows(kj), :]
            for h in range(2):
                d, tot = softplus_cumsum(head_logits(q2, kt, h))
                accumulate(h, qi, d, tot, vt, False)
            return kj - 1, jnp.min(carry_ref[:, rows(qi), :])

        lax.while_loop(alive, visit, (qi - 2, jnp.min(carry_ref[:, rows(qi), :])))
        return c

    if n_q > 2:
        @pl.when(jnp.min(carry_ref[:, 2 * t:, :]) < DEAD_CARRY)
        def _():
            lax.fori_loop(2, n_q, far_tiles, 0)

    def finish(i, c):
        r = rows(i)
        o = jnp.where(head_lanes[0], acc_ref[0, r, :], acc_ref[1, r, :])
        o_ref[r, :] = (o * g_ref[r, :].astype(F32)).astype(o_ref.dtype)
        return c

    lax.fori_loop(0, n_q, finish, 0)


def _attention(q, k, v, g, batch, seq):
    n, width = q.shape
    n_pairs = width // LANES
    t = ATT_TILE
    assert seq % t == 0 and seq // t >= 2

    idx = np.arange(t)
    tri = jnp.asarray(idx[:, None] >= idx[None, :], BF16)
    strict = idx[None, :] < idx[:, None]
    cap = jnp.asarray(np.where(strict, np.inf, MASKED_LOGIT).astype(np.float32))

    seq_spec = pl.BlockSpec((seq, LANES), lambda b, hp: (b, hp))
    tile_spec = pl.BlockSpec((t, t), lambda b, hp: (0, 0))
    return pl.pallas_call(
        _attn_kernel,
        grid=(batch, n_pairs),
        in_specs=[seq_spec, seq_spec, seq_spec, seq_spec, tile_spec, tile_spec],
        out_specs=seq_spec,
        out_shape=jax.ShapeDtypeStruct((n, width), BF16),
        scratch_shapes=[
            pltpu.VMEM((ATT_SLOTS, 2, t, t), F32),
            pltpu.VMEM((ATT_SLOTS, 2, t, t), F32),
            pltpu.VMEM((ATT_SLOTS, 2, t, LANES), F32),
            pltpu.VMEM((2, seq, LANES), F32),
            pltpu.VMEM((2, seq, LANES), F32),
        ],
        compiler_params=pltpu.CompilerParams(
            dimension_semantics=("arbitrary", "arbitrary"), vmem_limit_bytes=VMEM_LIMIT),
        name="stick_attn",
    )(q, k, v, g, tri, cap)


def _out_proj_kernel(conv_ref, attn_ref, x_ref, w_ref, gain_ref, bias_ref, o_ref, *, alpha):
    half = conv_ref.shape[1]
    for c in range(conv_ref.shape[0] // OUT_CHUNK):
        r = slice(c * OUT_CHUNK, (c + 1) * OUT_CHUNK)
        sub = jnp.dot(conv_ref[r, :], w_ref[0:half, :], preferred_element_type=F32)
        sub = sub + jnp.dot(attn_ref[r, :], w_ref[half:2 * half, :], preferred_element_type=F32)
        y = alpha * x_ref[r, :] + sub
        mu = jnp.mean(y, axis=-1, keepdims=True)
        yc = y - mu
        var = jnp.mean(yc * yc, axis=-1, keepdims=True)
        o_ref[r, :] = yc * lax.rsqrt(var + LN_EPS) * gain_ref[...] + bias_ref[...]


def _out_proj(conv, attn, x2, w_out_b, gain, bias, alpha):
    n, d = x2.shape
    half = conv.shape[1]
    tm = OUT_ROW_TILE
    return pl.pallas_call(
        functools.partial(_out_proj_kernel, alpha=alpha),
        grid=(n // tm,),
        in_specs=[
            pl.BlockSpec((tm, half), lambda i: (i, 0)),
            pl.BlockSpec((tm, half), lambda i: (i, 0)),
            pl.BlockSpec((tm, d), lambda i: (i, 0)),
            pl.BlockSpec(w_out_b.shape, lambda i: (0, 0)),
            pl.BlockSpec((1, d), lambda i: (0, 0)),
            pl.BlockSpec((1, d), lambda i: (0, 0)),
        ],
        out_specs=pl.BlockSpec((tm, d), lambda i: (i, 0)),
        out_shape=jax.ShapeDtypeStruct((n, d), F32),
        compiler_params=pltpu.CompilerParams(
            dimension_semantics=("arbitrary",), vmem_limit_bytes=VMEM_LIMIT),
        name="out_proj",
    )(conv, attn, x2, w_out_b, gain, bias)


def kernel(x, w_in, conv_w, w_out, ln_gain, ln_bias):
    batch, seq, d_model = x.shape
    depth = w_in.shape[0]
    alpha = (2.0 * depth) ** 0.25
    h = x.reshape(batch * seq, d_model)
    for layer in range(depth):
        conv, q, k, v, g = _in_proj(h, w_in[layer].astype(BF16), conv_w[layer], seq)
        attn = _attention(q, k, v, g, batch, seq)
        h = _out_proj(conv, attn, h, w_out[layer].astype(BF16),
                      ln_gain[layer][None, :], ln_bias[layer][None, :], alpha)
    return h.reshape(batch, seq, d_model)
```

```python
import functools

import jax
import jax.numpy as jnp
import numpy as np
from jax import lax
from jax.experimental import pallas as pl
from jax.experimental.pallas import tpu as pltpu

HEAD_DIM = 64
LN_EPS = 1e-5
LANES = 128
SUBLANES = 8
ROW_TILE = 512
OUT_ROW_TILE = 1024
OUT_CHUNK = 256
ATT_TILE = 256
ATT_SLOTS = 2
PROJ_CHUNK = 128
PROJ_AHEAD = 2
VMEM_LIMIT = 48 * 1024 * 1024
LOG2E = 1.4426950408889634
INV_LN2 = 1.4426950408889634
MASKED_LOGIT = -1e30
DEAD_CARRY = 160.0

F32 = jnp.float32
BF16 = jnp.bfloat16


def _sigmoid(z):
    return 1.0 / (1.0 + jnp.exp(-z))


def _conv_proj_kernel(x_ref, w_ref, cw_ref, conv_ref, xb_ref, ubuf_ref, *, tiles_per_seq, width):
    tm = x_ref.shape[0]
    xb = x_ref[...].astype(BF16)
    xb_ref[...] = xb

    def col(c):
        return jnp.dot(xb, w_ref[:, c * width:(c + 1) * width], preferred_element_type=F32)

    @pl.when(pl.program_id(0) % tiles_per_seq == 0)
    def _():
        ubuf_ref[0:SUBLANES, :] = jnp.zeros((SUBLANES, width), F32)

    u = col(1) * col(2)
    ubuf_ref[SUBLANES:SUBLANES + tm, :] = u
    u1 = ubuf_ref[SUBLANES - 1:SUBLANES - 1 + tm, :]
    u2 = ubuf_ref[SUBLANES - 2:SUBLANES - 2 + tm, :]
    y = cw_ref[2:3, :] * u + cw_ref[0:1, :] * u2 + cw_ref[1:2, :] * u1
    ubuf_ref[0:SUBLANES, :] = ubuf_ref[tm:tm + SUBLANES, :]

    z_c = col(3)
    conv_ref[...] = ((z_c * _sigmoid(z_c)) * (col(0) * y)).astype(conv_ref.dtype)


def _conv_proj(x2, w_in_b, conv_w, seq):
    n, d = x2.shape
    width = conv_w.shape[1]
    tm = ROW_TILE
    return pl.pallas_call(
        functools.partial(_conv_proj_kernel, tiles_per_seq=seq // tm, width=width),
        grid=(n // tm,),
        in_specs=[
            pl.BlockSpec((tm, d), lambda i: (i, 0)),
            pl.BlockSpec((d, 4 * width), lambda i: (0, 0)),
            pl.BlockSpec(conv_w.shape, lambda i: (0, 0)),
        ],
        out_specs=[pl.BlockSpec((tm, width), lambda i: (i, 0)),
                   pl.BlockSpec((tm, d), lambda i: (i, 0))],
        out_shape=[jax.ShapeDtypeStruct((n, width), BF16), jax.ShapeDtypeStruct((n, d), BF16)],
        scratch_shapes=[pltpu.VMEM((tm + 2 * SUBLANES, width), F32)],
        compiler_params=pltpu.CompilerParams(
            dimension_semantics=("arbitrary",), vmem_limit_bytes=VMEM_LIMIT),
        name="conv_proj",
    )(x2, w_in_b, conv_w)


def _near_schedule(n_tiles):
    steps = [(0, 0)]
    for i in range(1, n_tiles):
        steps += [(i, i), (i, i - 1)]
    return steps


def _attn_kernel(xb_ref, w_ref, tri_ref, cap_ref, o_ref,
                 q_s, k_s, v_s, g_s, z_buf, d_buf, tot_buf, carry_ref, acc_ref):
    t = ATT_TILE
    n_q = xb_ref.shape[0] // t
    n_slots = z_buf.shape[0]
    lane = lax.broadcasted_iota(jnp.int32, (t, LANES), 1)
    head_lanes = (lane < HEAD_DIM, lane >= HEAD_DIM)

    def rows(i):
        if isinstance(i, int):
            return slice(i * t, (i + 1) * t)
        return pl.ds(pl.multiple_of(i * t, t), t)

    def project(c):
        r = slice(c * PROJ_CHUNK, (c + 1) * PROJ_CHUNK)
        p = jnp.dot(xb_ref[r, :], w_ref[...], preferred_element_type=F32)
        q_s[r, :] = (p[:, 0:LANES] * (LOG2E * HEAD_DIM ** -0.5)).astype(BF16)
        k_s[r, :] = p[:, LANES:2 * LANES].astype(BF16)
        v_s[r, :] = p[:, 2 * LANES:3 * LANES].astype(BF16)
        z_a = p[:, 3 * LANES:4 * LANES]
        g_s[r, :] = (z_a * _sigmoid(z_a)).astype(BF16)

    def head_logits(q2, kt, h):
        qh = jnp.where(head_lanes[h], q2, jnp.zeros_like(q2))
        return lax.dot_general(qh, kt, (((1,), (1,)), ((), ())), preferred_element_type=F32)

    def softplus_cumsum(z):
        sp = jnp.maximum(z, 0.0) + jnp.log(1.0 + jnp.exp2(-jnp.abs(z))) * INV_LN2
        cin = jnp.dot(sp.astype(BF16), tri_ref[...], preferred_element_type=F32)
        return z - cin, jnp.broadcast_to(cin[:, 0:1], (t, LANES))

    def accumulate(h, qi, d, tot, vt, first):
        r = rows(qi)
        if first:
            w = jnp.exp2(d)
            carry_ref[h, r, :] = tot
        else:
            carry = carry_ref[h, r, :]
            w = jnp.exp2(d - jnp.concatenate([carry, carry], axis=1))
            carry_ref[h, r, :] = carry + tot
        pv = jnp.dot(w.astype(BF16), vt, preferred_element_type=F32)
        if first:
            acc_ref[h, r, :] = pv
        else:
            acc_ref[h, r, :] += pv

    steps = _near_schedule(n_q)

    def logits(s):
        qi, kj = steps[s]
        q2 = q_s[rows(qi), :]
        kt = k_s[rows(kj), :]
        for h in range(2):
            z = head_logits(q2, kt, h)
            z_buf[s % n_slots, h] = jnp.minimum(z, cap_ref[...]) if qi == kj else z

    def cumsum(s):
        for h in range(2):
            d_buf[s % n_slots, h], tot_buf[s % n_slots, h] = softplus_cumsum(z_buf[s % n_slots, h])

    def output(s):
        qi, kj = steps[s]
        vt = v_s[rows(kj), :]
        for h in range(2):
            accumulate(h, qi, d_buf[s % n_slots, h], tot_buf[s % n_slots, h], vt, qi == kj)

    n_chunks = xb_ref.shape[0] // PROJ_CHUNK
    projected = 0
    for s in range(len(steps) + 2):
        if s >= 2:
            output(s - 2)
        if 1 <= s <= len(steps):
            cumsum(s - 1)
        if s < len(steps):
            need = (steps[s][0] + 1) * (t // PROJ_CHUNK)
            for _ in range(max(need - projected, 0)):
                project(projected)
                projected += 1
            logits(s)
            if projected < min(need + PROJ_AHEAD, n_chunks):
                project(projected)
                projected += 1

    def far_tiles(qi, c):
        def alive(state):
            kj, low = state
            return jnp.logical_and(kj >= 0, low < DEAD_CARRY)

        def visit(state):
            kj, _ = state
            q2 = q_s[rows(qi), :]
            kt = k_s[rows(kj), :]
            vt = v_s[rows(kj), :]
            for h in range(2):
                d, tot = softplus_cumsum(head_logits(q2, kt, h))
                accumulate(h, qi, d, tot, vt, False)
            return kj - 1, jnp.min(carry_ref[:, rows(qi), :])

        lax.while_loop(alive, visit, (qi - 2, jnp.min(carry_ref[:, rows(qi), :])))
        return c

    if n_q > 2:
        @pl.when(jnp.min(carry_ref[:, 2 * t:, :]) < DEAD_CARRY)
        def _():
            lax.fori_loop(2, n_q, far_tiles, 0)

    def finish(i, c):
        r = rows(i)
        o = jnp.where(head_lanes[0], acc_ref[0, r, :], acc_ref[1, r, :])
        o_ref[r, :] = (o * g_s[r, :].astype(F32)).astype(o_ref.dtype)
        return c

    lax.fori_loop(0, n_q, finish, 0)


def _attention(xb, w_attn, batch, seq):
    n, d = xb.shape
    n_pairs = w_attn.shape[0]
    t = ATT_TILE
    assert seq % t == 0 and seq // t >= 2

    idx = np.arange(t)
    tri = jnp.asarray(idx[:, None] >= idx[None, :], BF16)
    strict = idx[None, :] < idx[:, None]
    cap = jnp.asarray(np.where(strict, np.inf, MASKED_LOGIT).astype(np.float32))

    tile_spec = pl.BlockSpec((t, t), lambda b, hp: (0, 0))
    seq_bf16 = pltpu.VMEM((seq, LANES), BF16)
    return pl.pallas_call(
        _attn_kernel,
        grid=(batch, n_pairs),
        in_specs=[pl.BlockSpec((seq, d), lambda b, hp: (b, 0)),
                  pl.BlockSpec((None, d, 4 * LANES), lambda b, hp: (hp, 0, 0)),
                  tile_spec, tile_spec],
        out_specs=pl.BlockSpec((seq, LANES), lambda b, hp: (b, hp)),
        out_shape=jax.ShapeDtypeStruct((n, n_pairs * LANES), BF16),
        scratch_shapes=[
            seq_bf16, seq_bf16, seq_bf16, seq_bf16,
            pltpu.VMEM((ATT_SLOTS, 2, t, t), F32),
            pltpu.VMEM((ATT_SLOTS, 2, t, t), F32),
            pltpu.VMEM((ATT_SLOTS, 2, t, LANES), F32),
            pltpu.VMEM((2, seq, LANES), F32),
            pltpu.VMEM((2, seq, LANES), F32),
        ],
        compiler_params=pltpu.CompilerParams(
            dimension_semantics=("arbitrary", "arbitrary"), vmem_limit_bytes=VMEM_LIMIT),
        name="stick_attn",
    )(xb, w_attn, tri, cap)


def _out_proj_kernel(conv_ref, attn_ref, x_ref, w_ref, gain_ref, bias_ref, o_ref, *, alpha):
    half = conv_ref.shape[1]
    for c in range(conv_ref.shape[0] // OUT_CHUNK):
        r = slice(c * OUT_CHUNK, (c + 1) * OUT_CHUNK)
        sub = jnp.dot(conv_ref[r, :], w_ref[0:half, :], preferred_element_type=F32)
        sub = sub + jnp.dot(attn_ref[r, :], w_ref[half:2 * half, :], preferred_element_type=F32)
        y = alpha * x_ref[r, :] + sub
        mu = jnp.mean(y, axis=-1, keepdims=True)
        yc = y - mu
        var = jnp.mean(yc * yc, axis=-1, keepdims=True)
        o_ref[r, :] = yc * lax.rsqrt(var + LN_EPS) * gain_ref[...] + bias_ref[...]


def _out_proj(conv, attn, x2, w_out_b, gain, bias, alpha):
    n, d = x2.shape
    half = conv.shape[1]
    tm = OUT_ROW_TILE
    return pl.pallas_call(
        functools.partial(_out_proj_kernel, alpha=alpha),
        grid=(n // tm,),
        in_specs=[
            pl.BlockSpec((tm, half), lambda i: (i, 0)),
            pl.BlockSpec((tm, half), lambda i: (i, 0)),
            pl.BlockSpec((tm, d), lambda i: (i, 0)),
            pl.BlockSpec(w_out_b.shape, lambda i: (0, 0)),
            pl.BlockSpec((1, d), lambda i: (0, 0)),
            pl.BlockSpec((1, d), lambda i: (0, 0)),
        ],
        out_specs=pl.BlockSpec((tm, d), lambda i: (i, 0)),
        out_shape=jax.ShapeDtypeStruct((n, d), F32),
        compiler_params=pltpu.CompilerParams(
            dimension_semantics=("arbitrary",), vmem_limit_bytes=VMEM_LIMIT),
        name="out_proj",
    )(conv, attn, x2, w_out_b, gain, bias)


def kernel(x, w_in, conv_w, w_out, ln_gain, ln_bias):
    batch, seq, d_model = x.shape
    depth = w_in.shape[0]
    alpha = (2.0 * depth) ** 0.25
    h = x.reshape(batch * seq, d_model)
    for layer in range(depth):
        w_in_b = w_in[layer].astype(BF16)
        width = conv_w.shape[-1]
        n_pairs = (w_in_b.shape[1] - 4 * width) // (4 * LANES)
        w_attn = w_in_b[:, 4 * width:].reshape(d_model, 4, n_pairs, LANES)
        w_attn = w_attn.transpose(2, 0, 1, 3).reshape(n_pairs, d_model, 4 * LANES)
        conv, xb = _conv_proj(h, w_in_b, conv_w[layer], seq)
        attn = _attention(xb, w_attn, batch, seq)
        h = _out_proj(conv, attn, h, w_out[layer].astype(BF16),
                      ln_gain[layer][None, :], ln_bias[layer][None, :], alpha)
    return h.reshape(batch, seq, d_model)
```

```python
import functools

import jax
import jax.numpy as jnp
import numpy as np
from jax import lax
from jax.experimental import pallas as pl
from jax.experimental.pallas import tpu as pltpu

HEAD_DIM = 64
LN_EPS = 1e-5
LANES = 128
SUBLANES = 8
ROW_TILE = 1024
IN_CHUNK = 256
OUT_ROW_TILE = 1024
OUT_CHUNK = 256
ATT_TILE = 256
VMEM_LIMIT = 48 * 1024 * 1024
LOG2E = 1.4426950408889634
INV_LN2 = 1.4426950408889634
MASKED_LOGIT = -1e30
DEAD_CARRY = 160.0

F32 = jnp.float32
BF16 = jnp.bfloat16


def _sigmoid(z):
    return 1.0 / (1.0 + jnp.exp(-z))


def _in_proj_kernel(x_ref, w_ref, cw_ref, conv_ref, q_ref, k_ref, v_ref, g_ref, ubuf_ref,
                    *, tiles_per_seq, width):
    tm = x_ref.shape[0]

    @pl.when(pl.program_id(0) % tiles_per_seq == 0)
    def _():
        ubuf_ref[0:SUBLANES, :] = jnp.zeros((SUBLANES, width), F32)

    for c in range(tm // IN_CHUNK):
        r = slice(c * IN_CHUNK, (c + 1) * IN_CHUNK)
        xb = x_ref[r, :].astype(BF16)

        def col(g, xb=xb):
            return jnp.dot(xb, w_ref[:, g * width:(g + 1) * width], preferred_element_type=F32)

        base = SUBLANES + c * IN_CHUNK
        u = col(1) * col(2)
        ubuf_ref[base:base + IN_CHUNK, :] = u
        u1 = ubuf_ref[base - 1:base - 1 + IN_CHUNK, :]
        u2 = ubuf_ref[base - 2:base - 2 + IN_CHUNK, :]
        y = cw_ref[2:3, :] * u + cw_ref[0:1, :] * u2 + cw_ref[1:2, :] * u1
        z_c = col(3)
        conv_ref[r, :] = ((z_c * _sigmoid(z_c)) * (col(0) * y)).astype(conv_ref.dtype)

        q_ref[r, :] = (col(4) * (LOG2E * HEAD_DIM ** -0.5)).astype(q_ref.dtype)
        k_ref[r, :] = col(5).astype(k_ref.dtype)
        v_ref[r, :] = col(6).astype(v_ref.dtype)
        z_a = col(7)
        g_ref[r, :] = (z_a * _sigmoid(z_a)).astype(g_ref.dtype)

    ubuf_ref[0:SUBLANES, :] = ubuf_ref[tm:tm + SUBLANES, :]


def _in_proj(x2, w_in_b, conv_w, seq):
    n, d = x2.shape
    width = w_in_b.shape[1] // 8
    tm = ROW_TILE
    out = jax.ShapeDtypeStruct((n, width), BF16)
    row_spec = pl.BlockSpec((tm, width), lambda i: (i, 0))
    return pl.pallas_call(
        functools.partial(_in_proj_kernel, tiles_per_seq=seq // tm, width=width),
        grid=(n // tm,),
        in_specs=[
            pl.BlockSpec((tm, d), lambda i: (i, 0)),
            pl.BlockSpec(w_in_b.shape, lambda i: (0, 0)),
            pl.BlockSpec(conv_w.shape, lambda i: (0, 0)),
        ],
        out_specs=[row_spec] * 5,
        out_shape=[out] * 5,
        scratch_shapes=[pltpu.VMEM((tm + 2 * SUBLANES, width), F32)],
        compiler_params=pltpu.CompilerParams(
            dimension_semantics=("arbitrary",), vmem_limit_bytes=VMEM_LIMIT),
        name="in_proj",
    )(x2, w_in_b, conv_w)


def _near_schedule(n_tiles):
    return [(i, i) for i in range(n_tiles)] + [(i, i - 1) for i in range(1, n_tiles)]


def _step_parts(qi, kj):
    t, h = ATT_TILE, ATT_TILE // 2
    if qi != kj:
        return [((qi * t, qi * t + t), (kj * t, kj * t + t), False)]
    return [((qi * t, qi * t + h), (kj * t, kj * t + h), True),
            ((qi * t + h, qi * t + t), (kj * t, kj * t + t), True)]


def _attn_kernel(q_ref, k_ref, v_ref, g_ref, tri_ref, cap_ref, o_ref, carry_ref, acc_ref):
    t = ATT_TILE
    n_q = q_ref.shape[0] // t

    def head_logits(q2, kt, h):
        lane = lax.broadcasted_iota(jnp.int32, q2.shape, 1)
        keep = lane < HEAD_DIM if h == 0 else lane >= HEAD_DIM
        qh = jnp.where(keep, q2, jnp.zeros_like(q2))
        return lax.dot_general(qh, kt, (((1,), (1,)), ((), ())), preferred_element_type=F32)

    def softplus_cumsum(z):
        nk = z.shape[1]
        sp = jnp.maximum(z, 0.0) + jnp.log(1.0 + jnp.exp2(-jnp.abs(z))) * INV_LN2
        cin = jnp.dot(sp.astype(BF16), tri_ref[0:nk, 0:nk], preferred_element_type=F32)
        return z - cin, jnp.broadcast_to(cin[:, 0:1], (z.shape[0], LANES))

    def accumulate(h, r, d, tot, vt, first):
        if first:
            w = jnp.exp2(d)
            carry_ref[h, r, :] = tot
        else:
            carry = carry_ref[h, r, :]
            w = jnp.exp2(d - jnp.concatenate([carry] * (d.shape[1] // LANES), axis=1))
            carry_ref[h, r, :] = carry + tot
        pv = jnp.dot(w.astype(BF16), vt, preferred_element_type=F32)
        if first:
            acc_ref[h, r, :] = pv
        else:
            acc_ref[h, r, :] += pv

    steps = _near_schedule(n_q)
    blocks = [[(slice(*q), slice(*k), m, h) for q, k, m in _step_parts(qi, kj) for h in range(2)]
              for qi, kj in steps]

    def logits(s):
        zs = []
        for qr, kr, masked, h in blocks[s]:
            z = head_logits(q_ref[qr, :], k_ref[kr, :], h)
            if masked:
                q0, k0 = qr.start % t, kr.start % t
                z = jnp.minimum(z, cap_ref[q0:q0 + z.shape[0], k0:k0 + z.shape[1]])
            zs.append(z)
        return zs

    def cumsum(zs):
        return [softplus_cumsum(z) for z in zs]

    def output(s, ds):
        for (qr, kr, masked, h), (d, tot) in zip(blocks[s], ds):
            accumulate(h, qr, d, tot, v_ref[kr, :], masked)

    z_vals, d_vals = {}, {}
    for s in range(len(steps) + 2):
        if s >= 2:
            output(s - 2, d_vals.pop(s - 2))
        if 1 <= s <= len(steps):
            d_vals[s - 1] = cumsum(z_vals.pop(s - 1))
        if s < len(steps):
            z_vals[s] = logits(s)

    def rows(i):
        return pl.ds(pl.multiple_of(i * t, t), t)

    def far_tiles(qi, c):
        def alive(state):
            kj, low = state
            return jnp.logical_and(kj >= 0, low < DEAD_CARRY)

        def visit(state):
            kj, _ = state
            q2 = q_ref[rows(qi), :]
            kt = k_ref[rows(kj), :]
            vt = v_ref[rows(kj), :]
            for h in range(2):
                d, tot = softplus_cumsum(head_logits(q2, kt, h))
                accumulate(h, rows(qi), d, tot, vt, False)
            return kj - 1, jnp.min(carry_ref[:, rows(qi), :])

        lax.while_loop(alive, visit, (qi - 2, jnp.min(carry_ref[:, rows(qi), :])))
        return c

    if n_q > 2:
        @pl.when(jnp.min(carry_ref[:, 2 * t:, :]) < DEAD_CARRY)
        def _():
            lax.fori_loop(2, n_q, far_tiles, 0)

    def finish(i, c):
        r = rows(i)
        lane = lax.broadcasted_iota(jnp.int32, (t, LANES), 1)
        o = jnp.where(lane < HEAD_DIM, acc_ref[0, r, :], acc_ref[1, r, :])
        o_ref[r, :] = (o * g_ref[r, :].astype(F32)).astype(o_ref.dtype)
        return c

    lax.fori_loop(0, n_q, finish, 0)


def _attention(q, k, v, g, batch, seq):
    n, width = q.shape
    n_pairs = width // LANES
    t = ATT_TILE
    assert seq % t == 0 and seq // t >= 2

    idx = np.arange(t)
    tri = jnp.asarray(idx[:, None] >= idx[None, :], BF16)
    strict = idx[None, :] < idx[:, None]
    cap = jnp.asarray(np.where(strict, np.inf, MASKED_LOGIT).astype(np.float32))

    seq_spec = pl.BlockSpec((seq, LANES), lambda b, hp: (b, hp))
    tile_spec = pl.BlockSpec((t, t), lambda b, hp: (0, 0))
    return pl.pallas_call(
        _attn_kernel,
        grid=(batch, n_pairs),
        in_specs=[seq_spec, seq_spec, seq_spec, seq_spec, tile_spec, tile_spec],
        out_specs=seq_spec,
        out_shape=jax.ShapeDtypeStruct((n, width), BF16),
        scratch_shapes=[
            pltpu.VMEM((2, seq, LANES), F32),
            pltpu.VMEM((2, seq, LANES), F32),
        ],
        compiler_params=pltpu.CompilerParams(
            dimension_semantics=("arbitrary", "arbitrary"), vmem_limit_bytes=VMEM_LIMIT),
        name="stick_attn",
    )(q, k, v, g, tri, cap)


def _out_proj_kernel(conv_ref, attn_ref, x_ref, w_ref, gain_ref, bias_ref, o_ref, *, alpha):
    half = conv_ref.shape[1]
    for c in range(conv_ref.shape[0] // OUT_CHUNK):
        r = slice(c * OUT_CHUNK, (c + 1) * OUT_CHUNK)
        sub = jnp.dot(conv_ref[r, :], w_ref[0:half, :], preferred_element_type=F32)
        sub = sub + jnp.dot(attn_ref[r, :], w_ref[half:2 * half, :], preferred_element_type=F32)
        y = alpha * x_ref[r, :] + sub
        mu = jnp.mean(y, axis=-1, keepdims=True)
        yc = y - mu
        var = jnp.mean(yc * yc, axis=-1, keepdims=True)
        o_ref[r, :] = yc * lax.rsqrt(var + LN_EPS) * gain_ref[...] + bias_ref[...]


def _out_proj(conv, attn, x2, w_out_b, gain, bias, alpha):
    n, d = x2.shape
    half = conv.shape[1]
    tm = OUT_ROW_TILE
    return pl.pallas_call(
        functools.partial(_out_proj_kernel, alpha=alpha),
        grid=(n // tm,),
        in_specs=[
            pl.BlockSpec((tm, half), lambda i: (i, 0)),
            pl.BlockSpec((tm, half), lambda i: (i, 0)),
            pl.BlockSpec((tm, d), lambda i: (i, 0)),
            pl.BlockSpec(w_out_b.shape, lambda i: (0, 0)),
            pl.BlockSpec((1, d), lambda i: (0, 0)),
            pl.BlockSpec((1, d), lambda i: (0, 0)),
        ],
        out_specs=pl.BlockSpec((tm, d), lambda i: (i, 0)),
        out_shape=jax.ShapeDtypeStruct((n, d), F32),
        compiler_params=pltpu.CompilerParams(
            dimension_semantics=("arbitrary",), vmem_limit_bytes=VMEM_LIMIT),
        name="out_proj",
    )(conv, attn, x2, w_out_b, gain, bias)


def kernel(x, w_in, conv_w, w_out, ln_gain, ln_bias):
    batch, seq, d_model = x.shape
    depth = w_in.shape[0]
    alpha = (2.0 * depth) ** 0.25
    h = x.reshape(batch * seq, d_model)
    for layer in range(depth):
        conv, q, k, v, g = _in_proj(h, w_in[layer].astype(BF16), conv_w[layer], seq)
        attn = _attention(q, k, v, g, batch, seq)
        h = _out_proj(conv, attn, h, w_out[layer].astype(BF16),
                      ln_gain[layer][None, :], ln_bias[layer][None, :], alpha)
    return h.reshape(batch, seq, d_model)
```

```python
import functools

import jax
import jax.numpy as jnp
import numpy as np
from jax import lax
from jax.experimental import pallas as pl
from jax.experimental.pallas import tpu as pltpu

HEAD_DIM = 64
LN_EPS = 1e-5
LANES = 128
SUBLANES = 8
ROW_TILE = 1024
IN_CHUNK = 256
OUT_ROW_TILE = 2048
OUT_CHUNK = 256
ATT_TILE = 256
VMEM_LIMIT = 48 * 1024 * 1024
LOG2E = 1.4426950408889634
INV_LN2 = 1.4426950408889634
MASKED_LOGIT = -1e30
DEAD_CARRY = 160.0
NEAR_ROWS = 192
CUMSUM_LAG = 1
OUTPUT_LAG = 2

F32 = jnp.float32
BF16 = jnp.bfloat16


def _sigmoid(z):
    return 1.0 / (1.0 + jnp.exp(-z))


def _in_proj_kernel(x_ref, w_ref, cw_ref, conv_ref, q_ref, k_ref, v_ref, g_ref, ubuf_ref, wb_ref,
                    *, tiles_per_seq, width):
    tm = x_ref.shape[0]

    @pl.when(pl.program_id(0) == 0)
    def _():
        for g in range(w_ref.shape[1] // width):
            wb_ref[:, g * width:(g + 1) * width] = w_ref[:, g * width:(g + 1) * width].astype(BF16)

    @pl.when(pl.program_id(0) % tiles_per_seq == 0)
    def _():
        ubuf_ref[0:SUBLANES, :] = jnp.zeros((SUBLANES, width), F32)

    for c in range(tm // IN_CHUNK):
        r = slice(c * IN_CHUNK, (c + 1) * IN_CHUNK)
        xb = x_ref[r, :].astype(BF16)

        def col(g, xb=xb):
            return jnp.dot(xb, wb_ref[:, g * width:(g + 1) * width], preferred_element_type=F32)

        base = SUBLANES + c * IN_CHUNK
        u = col(1) * col(2)
        ubuf_ref[base:base + IN_CHUNK, :] = u
        u1 = ubuf_ref[base - 1:base - 1 + IN_CHUNK, :]
        u2 = ubuf_ref[base - 2:base - 2 + IN_CHUNK, :]
        y = cw_ref[2:3, :] * u + cw_ref[0:1, :] * u2 + cw_ref[1:2, :] * u1
        z_c = col(3)
        conv_ref[r, :] = ((z_c * _sigmoid(z_c)) * (col(0) * y)).astype(conv_ref.dtype)

        z_a = col(7)
        g_ref[r, :] = (z_a * _sigmoid(z_a)).astype(g_ref.dtype)
        q_ref[r, :] = (col(4) * (LOG2E * HEAD_DIM ** -0.5)).astype(q_ref.dtype)
        k_ref[r, :] = col(5).astype(k_ref.dtype)
        v_ref[r, :] = col(6).astype(v_ref.dtype)

    ubuf_ref[0:SUBLANES, :] = ubuf_ref[tm:tm + SUBLANES, :]


def _in_proj(x2, w_in, conv_w, seq):
    n, d = x2.shape
    width = w_in.shape[1] // 8
    tm = ROW_TILE
    out = jax.ShapeDtypeStruct((n, width), BF16)
    row_spec = pl.BlockSpec((tm, width), lambda i: (i, 0))
    return pl.pallas_call(
        functools.partial(_in_proj_kernel, tiles_per_seq=seq // tm, width=width),
        grid=(n // tm,),
        in_specs=[
            pl.BlockSpec((tm, d), lambda i: (i, 0)),
            pl.BlockSpec(w_in.shape, lambda i: (0, 0), pipeline_mode=pl.Buffered(1)),
            pl.BlockSpec(conv_w.shape, lambda i: (0, 0)),
        ],
        out_specs=[row_spec] * 5,
        out_shape=[out] * 5,
        scratch_shapes=[pltpu.VMEM((tm + 2 * SUBLANES, width), F32),
                        pltpu.VMEM(w_in.shape, BF16)],
        compiler_params=pltpu.CompilerParams(
            dimension_semantics=("arbitrary",), vmem_limit_bytes=VMEM_LIMIT),
        name="in_proj",
    )(x2, w_in, conv_w)


def _near_schedule(n_tiles):
    return [(i, i) for i in range(n_tiles)] + [(i, i - 1) for i in range(1, n_tiles)]


def _step_parts(qi, kj):
    t, h = ATT_TILE, ATT_TILE // 2
    if qi != kj:
        return [((qi * t, qi * t + NEAR_ROWS), (kj * t, kj * t + t), False)]
    return [((qi * t, qi * t + h), (kj * t, kj * t + h), True),
            ((qi * t + h, qi * t + t), (kj * t, kj * t + t), True)]


def _attn_kernel(q_ref, k_ref, v_ref, g_ref, tri_ref, cap_ref, o_ref, carry_ref, acc_ref):
    t = ATT_TILE
    n_q = q_ref.shape[0] // t

    def head_logits(q2, kt, h):
        lane = lax.broadcasted_iota(jnp.int32, q2.shape, 1)
        keep = lane < HEAD_DIM if h == 0 else lane >= HEAD_DIM
        qh = jnp.where(keep, q2, jnp.zeros_like(q2))
        return lax.dot_general(qh, kt, (((1,), (1,)), ((), ())), preferred_element_type=F32)

    def softplus_cumsum(z):
        nk = z.shape[1]
        m = jnp.maximum(z, 0.0)
        sp = m + jnp.log(1.0 + jnp.exp2(z - (m + m))) * INV_LN2
        cin = jnp.dot(sp.astype(BF16), tri_ref[0:nk, 0:nk], preferred_element_type=F32)
        return z - cin, jnp.broadcast_to(cin[:, 0:1], (z.shape[0], LANES))

    def accumulate(h, r, d, tot, vt, first):
        if first:
            w = jnp.exp2(d)
            carry_ref[h, r, :] = tot
        else:
            carry = carry_ref[h, r, :]
            w = jnp.exp2(d - jnp.concatenate([carry] * (d.shape[1] // LANES), axis=1))
            carry_ref[h, r, :] = carry + tot
        pv = jnp.dot(w.astype(BF16), vt, preferred_element_type=F32)
        if first:
            acc_ref[h, r, :] = pv
        else:
            acc_ref[h, r, :] += pv

    steps = _near_schedule(n_q)
    blocks = [[(slice(*q), slice(*k), m, h) for q, k, m in _step_parts(qi, kj) for h in range(2)]
              for qi, kj in steps]

    def logits(s, e):
        qr, kr, masked, h = blocks[s][e]
        z = head_logits(q_ref[qr, :], k_ref[kr, :], h)
        if masked:
            q0, vis = qr.start % t, qr.start - kr.start
            capped = jnp.minimum(z[:, vis:], cap_ref[q0:q0 + z.shape[0], q0:q0 + z.shape[1] - vis])
            z = jnp.concatenate([z[:, :vis], capped], axis=1) if vis else capped
        return z

    def output(s, e, d, tot):
        qr, kr, masked, h = blocks[s][e]
        accumulate(h, qr, d, tot, v_ref[kr, :], masked)

    z_vals, d_vals = {}, {}
    for s in range(len(steps) + OUTPUT_LAG):
        for e in range(max(len(b) for b in blocks)):
            if (s - OUTPUT_LAG, e) in d_vals:
                output(s - OUTPUT_LAG, e, *d_vals.pop((s - OUTPUT_LAG, e)))
            if (s - CUMSUM_LAG, e) in z_vals:
                d_vals[s - CUMSUM_LAG, e] = softplus_cumsum(z_vals.pop((s - CUMSUM_LAG, e)))
            if s < len(steps) and e < len(blocks[s]):
                z_vals[s, e] = logits(s, e)
    assert not z_vals and not d_vals

    def rows(i):
        return pl.ds(pl.multiple_of(i * t, t), t)

    def far_tiles(qi, c):
        def alive(state):
            kj, low = state
            return jnp.logical_and(kj >= 0, low < DEAD_CARRY)

        def visit(state):
            kj, _ = state
            q2 = q_ref[rows(qi), :]
            kt = k_ref[rows(kj), :]
            vt = v_ref[rows(kj), :]
            for h in range(2):
                d, tot = softplus_cumsum(head_logits(q2, kt, h))
                accumulate(h, rows(qi), d, tot, vt, False)
            return kj - 1, jnp.min(carry_ref[:, rows(qi), :])

        lax.while_loop(alive, visit, (qi - 2, jnp.min(carry_ref[:, rows(qi), :])))
        return c

    def lower_rows(qi, c):
        r = pl.ds(pl.multiple_of(qi * t + NEAR_ROWS, t - NEAR_ROWS), t - NEAR_ROWS)

        @pl.when(jnp.min(carry_ref[:, r, :]) < DEAD_CARRY)
        def _():
            q2 = q_ref[r, :]
            kt = k_ref[rows(qi - 1), :]
            vt = v_ref[rows(qi - 1), :]
            for h in range(2):
                d, tot = softplus_cumsum(head_logits(q2, kt, h))
                accumulate(h, r, d, tot, vt, False)
        return c

    @pl.when(jnp.min(carry_ref[:, t:, :]) < DEAD_CARRY)
    def _():
        lax.fori_loop(1, n_q, lower_rows, 0)
        lax.fori_loop(2, n_q, far_tiles, 0)

    def finish(i, c):
        r = rows(i)
        lane = lax.broadcasted_iota(jnp.int32, (t, LANES), 1)
        o = jnp.where(lane < HEAD_DIM, acc_ref[0, r, :], acc_ref[1, r, :])
        o_ref[r, :] = (o * g_ref[r, :].astype(F32)).astype(o_ref.dtype)
        return c

    lax.fori_loop(0, n_q, finish, 0)


def _attention(q, k, v, g, batch, seq):
    n, width = q.shape
    n_pairs = width // LANES
    t = ATT_TILE
    assert seq % t == 0 and seq // t >= 2

    idx = np.arange(t)
    tri = jnp.asarray(idx[:, None] >= idx[None, :], BF16)
    strict = idx[None, :] < idx[:, None]
    cap = jnp.asarray(np.where(strict, np.inf, MASKED_LOGIT).astype(np.float32))

    seq_spec = pl.BlockSpec((seq, LANES), lambda b, hp: (b, hp))
    tile_spec = pl.BlockSpec((t, t), lambda b, hp: (0, 0))
    return pl.pallas_call(
        _attn_kernel,
        grid=(batch, n_pairs),
        in_specs=[seq_spec, seq_spec, seq_spec, seq_spec, tile_spec, tile_spec],
        out_specs=seq_spec,
        out_shape=jax.ShapeDtypeStruct((n, width), BF16),
        scratch_shapes=[
            pltpu.VMEM((2, seq, LANES), F32),
            pltpu.VMEM((2, seq, LANES), F32),
        ],
        compiler_params=pltpu.CompilerParams(
            dimension_semantics=("arbitrary", "arbitrary"), vmem_limit_bytes=VMEM_LIMIT),
        name="stick_attn",
    )(q, k, v, g, tri, cap)


def _out_proj_kernel(conv_ref, attn_ref, x_ref, w_ref, gain_ref, bias_ref, o_ref, wb_ref, *, alpha):
    half = conv_ref.shape[1]

    @pl.when(pl.program_id(0) == 0)
    def _():
        wb_ref[...] = w_ref[...].astype(BF16)

    for c in range(conv_ref.shape[0] // OUT_CHUNK):
        r = slice(c * OUT_CHUNK, (c + 1) * OUT_CHUNK)
        sub = jnp.dot(conv_ref[r, :], wb_ref[0:half, :], preferred_element_type=F32)
        sub = sub + jnp.dot(attn_ref[r, :], wb_ref[half:2 * half, :], preferred_element_type=F32)
        y = alpha * x_ref[r, :] + sub
        mu = jnp.mean(y, axis=-1, keepdims=True)
        yc = y - mu
        var = jnp.mean(yc * yc, axis=-1, keepdims=True)
        o_ref[r, :] = yc * lax.rsqrt(var + LN_EPS) * gain_ref[...] + bias_ref[...]


def _out_proj(conv, attn, x2, w_out, gain, bias, alpha):
    n, d = x2.shape
    half = conv.shape[1]
    tm = OUT_ROW_TILE
    return pl.pallas_call(
        functools.partial(_out_proj_kernel, alpha=alpha),
        grid=(n // tm,),
        in_specs=[
            pl.BlockSpec((tm, half), lambda i: (i, 0)),
            pl.BlockSpec((tm, half), lambda i: (i, 0)),
            pl.BlockSpec((tm, d), lambda i: (i, 0)),
            pl.BlockSpec(w_out.shape, lambda i: (0, 0), pipeline_mode=pl.Buffered(1)),
            pl.BlockSpec((1, d), lambda i: (0, 0)),
            pl.BlockSpec((1, d), lambda i: (0, 0)),
        ],
        out_specs=pl.BlockSpec((tm, d), lambda i: (i, 0)),
        out_shape=jax.ShapeDtypeStruct((n, d), F32),
        scratch_shapes=[pltpu.VMEM(w_out.shape, BF16)],
        compiler_params=pltpu.CompilerParams(
            dimension_semantics=("arbitrary",), vmem_limit_bytes=VMEM_LIMIT),
        name="out_proj",
    )(conv, attn, x2, w_out, gain, bias)


def kernel(x, w_in, conv_w, w_out, ln_gain, ln_bias):
    batch, seq, d_model = x.shape
    depth = w_in.shape[0]
    alpha = (2.0 * depth) ** 0.25
    h = x.reshape(batch * seq, d_model)
    for layer in range(depth):
        conv, q, k, v, g = _in_proj(h, w_in[layer], conv_w[layer], seq)
        attn = _attention(q, k, v, g, batch, seq)
        h = _out_proj(conv, attn, h, w_out[layer],
                      ln_gain[layer][None, :], ln_bias[layer][None, :], alpha)
    return h.reshape(batch, seq, d_model)
```

```python
import functools

import jax
import jax.numpy as jnp
import numpy as np
from jax import lax
from jax.experimental import pallas as pl
from jax.experimental.pallas import tpu as pltpu

HEAD_DIM = 64
LN_EPS = 1e-5
LANES = 128
SUBLANES = 8
ROW_TILE = 1024
IN_CHUNK = 256
OUT_ROW_TILE = 2048
OUT_CHUNK = 256
ATT_TILE = 256
VMEM_LIMIT = 48 * 1024 * 1024
LOG2E = 1.4426950408889634
INV_LN2 = 1.4426950408889634
MASKED_LOGIT = -1e30
DEAD_CARRY = 160.0
PREV_NEAR_ROWS = 192
PREV_FAR_ROWS = 64
CUMSUM_LAG = 1
OUTPUT_LAG = 2

F32 = jnp.float32
BF16 = jnp.bfloat16


def _sigmoid(z):
    return 1.0 / (1.0 + jnp.exp(-z))


def _in_proj_kernel(x_ref, w_ref, cw_ref, conv_ref, q_ref, k_ref, v_ref, g_ref, ubuf_ref, wb_ref,
                    *, tiles_per_seq, width):
    tm = x_ref.shape[0]

    @pl.when(pl.program_id(0) == 0)
    def _():
        for g in range(w_ref.shape[1] // width):
            wb_ref[:, g * width:(g + 1) * width] = w_ref[:, g * width:(g + 1) * width].astype(BF16)

    @pl.when(pl.program_id(0) % tiles_per_seq == 0)
    def _():
        ubuf_ref[0:SUBLANES, :] = jnp.zeros((SUBLANES, width), F32)

    for c in range(tm // IN_CHUNK):
        r = slice(c * IN_CHUNK, (c + 1) * IN_CHUNK)
        xb = x_ref[r, :].astype(BF16)

        def col(g, xb=xb):
            return jnp.dot(xb, wb_ref[:, g * width:(g + 1) * width], preferred_element_type=F32)

        base = SUBLANES + c * IN_CHUNK
        u = col(1) * col(2)
        ubuf_ref[base:base + IN_CHUNK, :] = u
        u1 = ubuf_ref[base - 1:base - 1 + IN_CHUNK, :]
        u2 = ubuf_ref[base - 2:base - 2 + IN_CHUNK, :]
        y = cw_ref[2:3, :] * u + cw_ref[0:1, :] * u2 + cw_ref[1:2, :] * u1
        z_c = col(3)
        conv_ref[r, :] = ((z_c * _sigmoid(z_c)) * (col(0) * y)).astype(conv_ref.dtype)

        z_a = col(7)
        g_ref[r, :] = (z_a * _sigmoid(z_a)).astype(g_ref.dtype)
        q_ref[r, :] = (col(4) * (LOG2E * HEAD_DIM ** -0.5)).astype(q_ref.dtype)
        k_ref[r, :] = col(5).astype(k_ref.dtype)
        v_ref[r, :] = col(6).astype(v_ref.dtype)

    ubuf_ref[0:SUBLANES, :] = ubuf_ref[tm:tm + SUBLANES, :]


def _in_proj(x2, w_in, conv_w, seq):
    n, d = x2.shape
    width = w_in.shape[1] // 8
    tm = ROW_TILE
    out = jax.ShapeDtypeStruct((n, width), BF16)
    row_spec = pl.BlockSpec((tm, width), lambda i: (i, 0))
    return pl.pallas_call(
        functools.partial(_in_proj_kernel, tiles_per_seq=seq // tm, width=width),
        grid=(n // tm,),
        in_specs=[
            pl.BlockSpec((tm, d), lambda i: (i, 0)),
            pl.BlockSpec(w_in.shape, lambda i: (0, 0), pipeline_mode=pl.Buffered(1)),
            pl.BlockSpec(conv_w.shape, lambda i: (0, 0)),
        ],
        out_specs=[row_spec] * 5,
        out_shape=[out] * 5,
        scratch_shapes=[pltpu.VMEM((tm + 2 * SUBLANES, width), F32),
                        pltpu.VMEM(w_in.shape, BF16)],
        compiler_params=pltpu.CompilerParams(
            dimension_semantics=("arbitrary",), vmem_limit_bytes=VMEM_LIMIT),
        name="in_proj",
    )(x2, w_in, conv_w)


def _near_schedule(n_tiles):
    t, h = ATT_TILE, ATT_TILE // 2
    diag = [[((i * t, i * t + h), (i * t, i * t + h), True),
             ((i * t + h, i * t + t), (i * t, i * t + t), True)] for i in range(n_tiles)]
    near = {i: ((i * t, i * t + PREV_NEAR_ROWS), (i * t - h, i * t), False)
            for i in range(1, n_tiles)}
    far = {i: ((i * t, i * t + PREV_FAR_ROWS), (i * t - t, i * t - h), False)
           for i in range(1, n_tiles)}
    prev = [[near[i]] + ([far[i - 1]] if i > 1 else []) for i in range(1, n_tiles)]
    return diag + prev + [[far[n_tiles - 1]]]


def _attn_kernel(q_ref, k_ref, v_ref, g_ref, tri_ref, cap_ref, o_ref, carry_ref, acc_ref):
    t = ATT_TILE
    n_q = q_ref.shape[0] // t

    def head_logits(q2, kt, h):
        lane = lax.broadcasted_iota(jnp.int32, q2.shape, 1)
        keep = lane < HEAD_DIM if h == 0 else lane >= HEAD_DIM
        qh = jnp.where(keep, q2, jnp.zeros_like(q2))
        return lax.dot_general(qh, kt, (((1,), (1,)), ((), ())), preferred_element_type=F32)

    def softplus_cumsum(z):
        nk = z.shape[1]
        m = jnp.maximum(z, 0.0)
        sp = m + jnp.log(1.0 + jnp.exp2(z - (m + m))) * INV_LN2
        cin = jnp.dot(sp.astype(BF16), tri_ref[0:nk, 0:nk], preferred_element_type=F32)
        return z - cin, jnp.broadcast_to(cin[:, 0:1], (z.shape[0], LANES))

    def accumulate(h, r, d, tot, vt, first):
        if first:
            w = jnp.exp2(d)
            carry_ref[h, r, :] = tot
        else:
            carry = carry_ref[h, r, :]
            w = jnp.exp2(d - jnp.concatenate([carry] * (d.shape[1] // LANES), axis=1))
            carry_ref[h, r, :] = carry + tot
        pv = jnp.dot(w.astype(BF16), vt, preferred_element_type=F32)
        if first:
            acc_ref[h, r, :] = pv
        else:
            acc_ref[h, r, :] += pv

    steps = _near_schedule(n_q)
    blocks = [[(slice(*q), slice(*k), m, h) for q, k, m in step for h in range(2)]
              for step in steps]

    def logits(s, e):
        qr, kr, masked, h = blocks[s][e]
        z = head_logits(q_ref[qr, :], k_ref[kr, :], h)
        if masked:
            q0, vis = qr.start % t, qr.start - kr.start
            capped = jnp.minimum(z[:, vis:], cap_ref[q0:q0 + z.shape[0], q0:q0 + z.shape[1] - vis])
            z = jnp.concatenate([z[:, :vis], capped], axis=1) if vis else capped
        return z

    def output(s, e, d, tot):
        qr, kr, masked, h = blocks[s][e]
        accumulate(h, qr, d, tot, v_ref[kr, :], masked)

    z_vals, d_vals = {}, {}
    for s in range(len(steps) + OUTPUT_LAG):
        for e in range(max(len(b) for b in blocks)):
            if (s - OUTPUT_LAG, e) in d_vals:
                output(s - OUTPUT_LAG, e, *d_vals.pop((s - OUTPUT_LAG, e)))
            if (s - CUMSUM_LAG, e) in z_vals:
                d_vals[s - CUMSUM_LAG, e] = softplus_cumsum(z_vals.pop((s - CUMSUM_LAG, e)))
            if s < len(steps) and e < len(blocks[s]):
                z_vals[s, e] = logits(s, e)
    assert not z_vals and not d_vals

    def rows(i):
        return pl.ds(pl.multiple_of(i * t, t), t)

    def far_tiles(qi, c):
        def alive(state):
            kj, low = state
            return jnp.logical_and(kj >= 0, low < DEAD_CARRY)

        def visit(state):
            kj, _ = state
            q2 = q_ref[rows(qi), :]
            kt = k_ref[rows(kj), :]
            vt = v_ref[rows(kj), :]
            for h in range(2):
                d, tot = softplus_cumsum(head_logits(q2, kt, h))
                accumulate(h, rows(qi), d, tot, vt, False)
            return kj - 1, jnp.min(carry_ref[:, rows(qi), :])

        lax.while_loop(alive, visit, (qi - 2, jnp.min(carry_ref[:, rows(qi), :])))
        return c

    def visit_if_alive(r, kr):
        @pl.when(jnp.min(carry_ref[:, r, :]) < DEAD_CARRY)
        def _():
            q2 = q_ref[r, :]
            kt = k_ref[kr, :]
            vt = v_ref[kr, :]
            for h in range(2):
                d, tot = softplus_cumsum(head_logits(q2, kt, h))
                accumulate(h, r, d, tot, vt, False)

    def skipped_rows(qi, c):
        half, base = t // 2, qi * t
        n_mid, n_low = PREV_NEAR_ROWS - PREV_FAR_ROWS, t - PREV_NEAR_ROWS
        mid = pl.ds(pl.multiple_of(base + PREV_FAR_ROWS, PREV_FAR_ROWS), n_mid)
        low = pl.ds(pl.multiple_of(base + PREV_NEAR_ROWS, n_low), n_low)
        visit_if_alive(mid, pl.ds(pl.multiple_of(base - t, half), half))
        visit_if_alive(low, rows(qi - 1))
        return c

    @pl.when(jnp.min(carry_ref[:, t:, :]) < DEAD_CARRY)
    def _():
        lax.fori_loop(1, n_q, skipped_rows, 0)
        lax.fori_loop(2, n_q, far_tiles, 0)

    def finish(i, c):
        r = rows(i)
        lane = lax.broadcasted_iota(jnp.int32, (t, LANES), 1)
        o = jnp.where(lane < HEAD_DIM, acc_ref[0, r, :], acc_ref[1, r, :])
        o_ref[r, :] = (o * g_ref[r, :].astype(F32)).astype(o_ref.dtype)
        return c

    lax.fori_loop(0, n_q, finish, 0)


def _attention(q, k, v, g, batch, seq):
    n, width = q.shape
    n_pairs = width // LANES
    t = ATT_TILE
    assert seq % t == 0 and seq // t >= 2

    idx = np.arange(t)
    tri = jnp.asarray(idx[:, None] >= idx[None, :], BF16)
    strict = idx[None, :] < idx[:, None]
    cap = jnp.asarray(np.where(strict, np.inf, MASKED_LOGIT).astype(np.float32))

    seq_spec = pl.BlockSpec((seq, LANES), lambda b, hp: (b, hp))
    tile_spec = pl.BlockSpec((t, t), lambda b, hp: (0, 0))
    return pl.pallas_call(
        _attn_kernel,
        grid=(batch, n_pairs),
        in_specs=[seq_spec, seq_spec, seq_spec, seq_spec, tile_spec, tile_spec],
        out_specs=seq_spec,
        out_shape=jax.ShapeDtypeStruct((n, width), BF16),
        scratch_shapes=[
            pltpu.VMEM((2, seq, LANES), F32),
            pltpu.VMEM((2, seq, LANES), F32),
        ],
        compiler_params=pltpu.CompilerParams(
            dimension_semantics=("arbitrary", "arbitrary"), vmem_limit_bytes=VMEM_LIMIT),
        name="stick_attn",
    )(q, k, v, g, tri, cap)


def _out_proj_kernel(conv_ref, attn_ref, x_ref, w_ref, gain_ref, bias_ref, o_ref, wb_ref, *, alpha):
    half = conv_ref.shape[1]

    @pl.when(pl.program_id(0) == 0)
    def _():
        wb_ref[...] = w_ref[...].astype(BF16)

    for c in range(conv_ref.shape[0] // OUT_CHUNK):
        r = slice(c * OUT_CHUNK, (c + 1) * OUT_CHUNK)
        sub = jnp.dot(conv_ref[r, :], wb_ref[0:half, :], preferred_element_type=F32)
        sub = sub + jnp.dot(attn_ref[r, :], wb_ref[half:2 * half, :], preferred_element_type=F32)
        y = alpha * x_ref[r, :] + sub
        mu = jnp.mean(y, axis=-1, keepdims=True)
        yc = y - mu
        var = jnp.mean(yc * yc, axis=-1, keepdims=True)
        o_ref[r, :] = yc * lax.rsqrt(var + LN_EPS) * gain_ref[...] + bias_ref[...]


def _out_proj(conv, attn, x2, w_out, gain, bias, alpha):
    n, d = x2.shape
    half = conv.shape[1]
    tm = OUT_ROW_TILE
    return pl.pallas_call(
        functools.partial(_out_proj_kernel, alpha=alpha),
        grid=(n // tm,),
        in_specs=[
            pl.BlockSpec((tm, half), lambda i: (i, 0)),
            pl.BlockSpec((tm, half), lambda i: (i, 0)),
            pl.BlockSpec((tm, d), lambda i: (i, 0)),
            pl.BlockSpec(w_out.shape, lambda i: (0, 0), pipeline_mode=pl.Buffered(1)),
            pl.BlockSpec((1, d), lambda i: (0, 0)),
            pl.BlockSpec((1, d), lambda i: (0, 0)),
        ],
        out_specs=pl.BlockSpec((tm, d), lambda i: (i, 0)),
        out_shape=jax.ShapeDtypeStruct((n, d), F32),
        scratch_shapes=[pltpu.VMEM(w_out.shape, BF16)],
        compiler_params=pltpu.CompilerParams(
            dimension_semantics=("arbitrary",), vmem_limit_bytes=VMEM_LIMIT),
        name="out_proj",
    )(conv, attn, x2, w_out, gain, bias)


def kernel(x, w_in, conv_w, w_out, ln_gain, ln_bias):
    batch, seq, d_model = x.shape
    depth = w_in.shape[0]
    alpha = (2.0 * depth) ** 0.25
    h = x.reshape(batch * seq, d_model)
    for layer in range(depth):
        conv, q, k, v, g = _in_proj(h, w_in[layer], conv_w[layer], seq)
        attn = _attention(q, k, v, g, batch, seq)
        h = _out_proj(conv, attn, h, w_out[layer],
                      ln_gain[layer][None, :], ln_bias[layer][None, :], alpha)
    return h.reshape(batch, seq, d_model)
```

```python
import functools

import jax
import jax.numpy as jnp
import numpy as np
from jax import lax
from jax.experimental import pallas as pl
from jax.experimental.pallas import tpu as pltpu

HEAD_DIM = 64
LN_EPS = 1e-5
LANES = 128
SUBLANES = 8
ROW_TILE = 1024
IN_CHUNK = 256
OUT_ROW_TILE = 2048
OUT_CHUNK = 256
ATT_TILE = 256
VMEM_LIMIT = 48 * 1024 * 1024
LOG2E = 1.4426950408889634
INV_LN2 = 1.4426950408889634
MASKED_LOGIT = -1e30
DEAD_CARRY = 160.0
NEAR_ROWS = 192
CUMSUM_LAG = 1
OUTPUT_LAG = 2

F32 = jnp.float32
BF16 = jnp.bfloat16


def _sigmoid(z):
    return 1.0 / (1.0 + jnp.exp(-z))


def _in_proj_kernel(x_ref, w_ref, cw_ref, conv_ref, q_ref, k_ref, v_ref, g_ref, ubuf_ref, wb_ref,
                    *, tiles_per_seq, width):
    tm = x_ref.shape[0]

    @pl.when(pl.program_id(0) == 0)
    def _():
        for g in range(w_ref.shape[1] // width):
            wb_ref[:, g * width:(g + 1) * width] = w_ref[:, g * width:(g + 1) * width].astype(BF16)

    @pl.when(pl.program_id(0) % tiles_per_seq == 0)
    def _():
        ubuf_ref[0:SUBLANES, :] = jnp.zeros((SUBLANES, width), F32)

    for c in range(tm // IN_CHUNK):
        r = slice(c * IN_CHUNK, (c + 1) * IN_CHUNK)
        xb = x_ref[r, :].astype(BF16)

        def col(g, xb=xb):
            return jnp.dot(xb, wb_ref[:, g * width:(g + 1) * width], preferred_element_type=F32)

        base = SUBLANES + c * IN_CHUNK
        u = col(1) * col(2)
        ubuf_ref[base:base + IN_CHUNK, :] = u
        u1 = ubuf_ref[base - 1:base - 1 + IN_CHUNK, :]
        u2 = ubuf_ref[base - 2:base - 2 + IN_CHUNK, :]
        y = cw_ref[2:3, :] * u + cw_ref[0:1, :] * u2 + cw_ref[1:2, :] * u1
        z_c = col(3)
        conv_ref[r, :] = ((z_c * _sigmoid(z_c)) * (col(0) * y)).astype(conv_ref.dtype)

        z_a = col(7)
        g_ref[r, :] = (z_a * _sigmoid(z_a)).astype(g_ref.dtype)
        q_ref[r, :] = (col(4) * (LOG2E * HEAD_DIM ** -0.5)).astype(q_ref.dtype)
        k_ref[r, :] = col(5).astype(k_ref.dtype)
        v_ref[r, :] = col(6).astype(v_ref.dtype)

    ubuf_ref[0:SUBLANES, :] = ubuf_ref[tm:tm + SUBLANES, :]


def _in_proj(x2, w_in, conv_w, seq):
    n, d = x2.shape
    width = w_in.shape[1] // 8
    tm = ROW_TILE
    out = jax.ShapeDtypeStruct((n, width), BF16)
    row_spec = pl.BlockSpec((tm, width), lambda i: (i, 0))
    return pl.pallas_call(
        functools.partial(_in_proj_kernel, tiles_per_seq=seq // tm, width=width),
        grid=(n // tm,),
        in_specs=[
            pl.BlockSpec((tm, d), lambda i: (i, 0)),
            pl.BlockSpec(w_in.shape, lambda i: (0, 0), pipeline_mode=pl.Buffered(1)),
            pl.BlockSpec(conv_w.shape, lambda i: (0, 0)),
        ],
        out_specs=[row_spec] * 5,
        out_shape=[out] * 5,
        scratch_shapes=[pltpu.VMEM((tm + 2 * SUBLANES, width), F32),
                        pltpu.VMEM(w_in.shape, BF16)],
        compiler_params=pltpu.CompilerParams(
            dimension_semantics=("arbitrary",), vmem_limit_bytes=VMEM_LIMIT),
        name="in_proj",
    )(x2, w_in, conv_w)


def _near_schedule(n_tiles):
    return [(i, i) for i in range(n_tiles)] + [(i, i - 1) for i in range(1, n_tiles)]


def _step_parts(qi, kj):
    t, h = ATT_TILE, ATT_TILE // 2
    if qi != kj:
        return [((qi * t, qi * t + NEAR_ROWS), (kj * t, kj * t + t), False)]
    return [((qi * t, qi * t + h), (kj * t, kj * t + h), True),
            ((qi * t + h, qi * t + t), (kj * t, kj * t + t), True)]


def _attn_kernel(q_ref, k_ref, v_ref, g_ref, tri_ref, cap_ref, o_ref, carry_ref, acc_ref):
    t = ATT_TILE
    n_q = q_ref.shape[0] // t

    def head_logits(q2, kt, h):
        lane = lax.broadcasted_iota(jnp.int32, q2.shape, 1)
        keep = lane < HEAD_DIM if h == 0 else lane >= HEAD_DIM
        qh = jnp.where(keep, q2, jnp.zeros_like(q2))
        return lax.dot_general(qh, kt, (((1,), (1,)), ((), ())), preferred_element_type=F32)

    def softplus_cumsum(z):
        nk = z.shape[1]
        m = jnp.maximum(z, 0.0)
        sp = m + jnp.log(1.0 + jnp.exp2(z - (m + m))) * INV_LN2
        cin = jnp.dot(sp.astype(BF16), tri_ref[0:nk, 0:nk], preferred_element_type=F32)
        return z - cin, jnp.broadcast_to(cin[:, 0:1], (z.shape[0], LANES))

    def weights(h, r, d, tot, first):
        if first:
            carry_ref[h, r, :] = tot
            return jnp.exp2(d).astype(BF16)
        carry = carry_ref[h, r, :]
        carry_ref[h, r, :] = carry + tot
        return jnp.exp2(d - jnp.concatenate([carry] * (d.shape[1] // LANES), axis=1)).astype(BF16)

    def add_pv(r, ws, vt, first):
        lane = lax.broadcasted_iota(jnp.int32, vt.shape, 1)
        zero = jnp.zeros_like(vt)
        v2 = jnp.concatenate([jnp.where(lane < HEAD_DIM, vt, zero),
                              jnp.where(lane >= HEAD_DIM, vt, zero)], axis=0)
        pv = jnp.dot(jnp.concatenate(ws, axis=1), v2, preferred_element_type=F32)
        if first:
            acc_ref[r, :] = pv
        else:
            acc_ref[r, :] += pv

    steps = _near_schedule(n_q)
    blocks = [[(slice(*q), slice(*k), m, h) for q, k, m in _step_parts(qi, kj) for h in range(2)]
              for qi, kj in steps]

    def logits(s, e):
        qr, kr, masked, h = blocks[s][e]
        z = head_logits(q_ref[qr, :], k_ref[kr, :], h)
        if masked:
            q0, vis = qr.start % t, qr.start - kr.start
            capped = jnp.minimum(z[:, vis:], cap_ref[q0:q0 + z.shape[0], q0:q0 + z.shape[1] - vis])
            z = jnp.concatenate([z[:, :vis], capped], axis=1) if vis else capped
        return z

    w_vals = {}

    def output(s, e, d, tot):
        qr, kr, masked, h = blocks[s][e]
        w = weights(h, qr, d, tot, masked)
        if h == 0:
            w_vals[s, e] = w
        else:
            add_pv(qr, [w_vals.pop((s, e - 1)), w], v_ref[kr, :], masked)

    z_vals, d_vals = {}, {}
    for s in range(len(steps) + OUTPUT_LAG):
        for e in range(max(len(b) for b in blocks)):
            if (s - OUTPUT_LAG, e) in d_vals:
                output(s - OUTPUT_LAG, e, *d_vals.pop((s - OUTPUT_LAG, e)))
            if (s - CUMSUM_LAG, e) in z_vals:
                d_vals[s - CUMSUM_LAG, e] = softplus_cumsum(z_vals.pop((s - CUMSUM_LAG, e)))
            if s < len(steps) and e < len(blocks[s]):
                z_vals[s, e] = logits(s, e)
    assert not z_vals and not d_vals and not w_vals

    def rows(i):
        return pl.ds(pl.multiple_of(i * t, t), t)

    def far_tiles(qi, c):
        def alive(state):
            kj, low = state
            return jnp.logical_and(kj >= 0, low < DEAD_CARRY)

        def visit(state):
            kj, _ = state
            q2 = q_ref[rows(qi), :]
            kt = k_ref[rows(kj), :]
            ws = [weights(h, rows(qi), *softplus_cumsum(head_logits(q2, kt, h)), False)
                  for h in range(2)]
            add_pv(rows(qi), ws, v_ref[rows(kj), :], False)
            return kj - 1, jnp.min(carry_ref[:, rows(qi), :])

        lax.while_loop(alive, visit, (qi - 2, jnp.min(carry_ref[:, rows(qi), :])))
        return c

    def lower_rows(qi, c):
        r = pl.ds(pl.multiple_of(qi * t + NEAR_ROWS, t - NEAR_ROWS), t - NEAR_ROWS)

        @pl.when(jnp.min(carry_ref[:, r, :]) < DEAD_CARRY)
        def _():
            q2 = q_ref[r, :]
            kt = k_ref[rows(qi - 1), :]
            ws = [weights(h, r, *softplus_cumsum(head_logits(q2, kt, h)), False) for h in range(2)]
            add_pv(r, ws, v_ref[rows(qi - 1), :], False)
        return c

    @pl.when(jnp.min(carry_ref[:, t:, :]) < DEAD_CARRY)
    def _():
        lax.fori_loop(1, n_q, lower_rows, 0)
        lax.fori_loop(2, n_q, far_tiles, 0)

    def finish(i, c):
        r = rows(i)
        o_ref[r, :] = (acc_ref[r, :] * g_ref[r, :].astype(F32)).astype(o_ref.dtype)
        return c

    lax.fori_loop(0, n_q, finish, 0)


def _attention(q, k, v, g, batch, seq):
    n, width = q.shape
    n_pairs = width // LANES
    t = ATT_TILE
    assert seq % t == 0 and seq // t >= 2

    idx = np.arange(t)
    tri = jnp.asarray(idx[:, None] >= idx[None, :], BF16)
    strict = idx[None, :] < idx[:, None]
    cap = jnp.asarray(np.where(strict, np.inf, MASKED_LOGIT).astype(np.float32))

    seq_spec = pl.BlockSpec((seq, LANES), lambda b, hp: (b, hp))
    tile_spec = pl.BlockSpec((t, t), lambda b, hp: (0, 0))
    return pl.pallas_call(
        _attn_kernel,
        grid=(batch, n_pairs),
        in_specs=[seq_spec, seq_spec, seq_spec, seq_spec, tile_spec, tile_spec],
        out_specs=seq_spec,
        out_shape=jax.ShapeDtypeStruct((n, width), BF16),
        scratch_shapes=[
            pltpu.VMEM((2, seq, LANES), F32),
            pltpu.VMEM((seq, LANES), F32),
        ],
        compiler_params=pltpu.CompilerParams(
            dimension_semantics=("arbitrary", "arbitrary"), vmem_limit_bytes=VMEM_LIMIT),
        name="stick_attn",
    )(q, k, v, g, tri, cap)


def _out_proj_kernel(conv_ref, attn_ref, x_ref, w_ref, gain_ref, bias_ref, o_ref, wb_ref, *, alpha):
    half = conv_ref.shape[1]

    @pl.when(pl.program_id(0) == 0)
    def _():
        wb_ref[...] = w_ref[...].astype(BF16)

    for c in range(conv_ref.shape[0] // OUT_CHUNK):
        r = slice(c * OUT_CHUNK, (c + 1) * OUT_CHUNK)
        sub = jnp.dot(conv_ref[r, :], wb_ref[0:half, :], preferred_element_type=F32)
        sub = sub + jnp.dot(attn_ref[r, :], wb_ref[half:2 * half, :], preferred_element_type=F32)
        y = alpha * x_ref[r, :] + sub
        mu = jnp.mean(y, axis=-1, keepdims=True)
        yc = y - mu
        var = jnp.mean(yc * yc, axis=-1, keepdims=True)
        o_ref[r, :] = yc * lax.rsqrt(var + LN_EPS) * gain_ref[...] + bias_ref[...]


def _out_proj(conv, attn, x2, w_out, gain, bias, alpha):
    n, d = x2.shape
    half = conv.shape[1]
    tm = OUT_ROW_TILE
    return pl.pallas_call(
        functools.partial(_out_proj_kernel, alpha=alpha),
        grid=(n // tm,),
        in_specs=[
            pl.BlockSpec((tm, half), lambda i: (i, 0)),
            pl.BlockSpec((tm, half), lambda i: (i, 0)),
            pl.BlockSpec((tm, d), lambda i: (i, 0)),
            pl.BlockSpec(w_out.shape, lambda i: (0, 0), pipeline_mode=pl.Buffered(1)),
            pl.BlockSpec((1, d), lambda i: (0, 0)),
            pl.BlockSpec((1, d), lambda i: (0, 0)),
        ],
        out_specs=pl.BlockSpec((tm, d), lambda i: (i, 0)),
        out_shape=jax.ShapeDtypeStruct((n, d), F32),
        scratch_shapes=[pltpu.VMEM(w_out.shape, BF16)],
        compiler_params=pltpu.CompilerParams(
            dimension_semantics=("arbitrary",), vmem_limit_bytes=VMEM_LIMIT),
        name="out_proj",
    )(conv, attn, x2, w_out, gain, bias)


def kernel(x, w_in, conv_w, w_out, ln_gain, ln_bias):
    batch, seq, d_model = x.shape
    depth = w_in.shape[0]
    alpha = (2.0 * depth) ** 0.25
    h = x.reshape(batch * seq, d_model)
    for layer in range(depth):
        conv, q, k, v, g = _in_proj(h, w_in[layer], conv_w[layer], seq)
        attn = _attention(q, k, v, g, batch, seq)
        h = _out_proj(conv, attn, h, w_out[layer],
                      ln_gain[layer][None, :], ln_bias[layer][None, :], alpha)
    return h.reshape(batch, seq, d_model)
```
